```python
import math
import jax
import jax.numpy as jnp
from jax import lax
import numpy as np


D_MODEL = 4096
BATCH = 1
SEQ = 16384
DEPTH = 4

HEAD_DIM = 64
CONV_CH = D_MODEL // 4
CONV_WIDTH = 31
RWKV_DIM = D_MODEL // 4
RWKV_HEADS = RWKV_DIM // HEAD_DIM
W_LORA = 64
A_LORA = 64
G_LORA = 160
RWKV_LNX_EPS = 64e-5
ATT_DIM = D_MODEL // 2
ATT_Q_HEADS = ATT_DIM // HEAD_DIM
ATT_KV_HEADS = ATT_Q_HEADS // 8
ATT_GROUP = ATT_Q_HEADS // ATT_KV_HEADS
ATT_KV_DIM = ATT_KV_HEADS * HEAD_DIM
WINDOW = 128
BLOCK = 128
N_BUCKETS = 32
MAX_DISTANCE = 128
MIX_DIM = CONV_CH + RWKV_DIM + ATT_DIM
RWKV_COLS = 3 * RWKV_DIM + W_LORA + A_LORA + G_LORA
IN_COLS = 2 * CONV_CH + RWKV_COLS + ATT_DIM + 2 * ATT_KV_DIM
D_FF = 256 * ((8 * D_MODEL // 3 + 255) // 256)
FFN_CONV_WIDTH = 3
RMS_EPS = 1e-6
LN_EPS = 1e-5

kernel_name = 'hybrid_conformer_rwkv7_swa_sink_block'


def rmsnorm(x, w):
    xf = x.astype(jnp.float32)
    y = xf * lax.rsqrt(jnp.mean(xf * xf, axis=-1, keepdims=True) + RMS_EPS)
    return (y * w.astype(jnp.float32)).astype(x.dtype)


def normalize_last(x, eps):
    xf = x.astype(jnp.float32)
    mu = jnp.mean(xf, axis=-1, keepdims=True)
    var = jnp.mean(jnp.square(xf - mu), axis=-1, keepdims=True)
    return ((xf - mu) * lax.rsqrt(var + eps)).astype(x.dtype)


def causal_dwconv(x, w, b):
    k_width, ch = w.shape
    y = lax.conv_general_dilated(x, w[:, None, :], window_strides=(1,), padding=[(k_width - 1, 0)],
                                 dimension_numbers=('NWC', 'WIO', 'NWC'), feature_group_count=ch)
    return y + b


def token_shift(p):
    return jnp.pad(p, ((0, 0), (1, 0), (0, 0)))[:, :-1]


def conformer_conv_mixer(p, dw_w, dw_b, ln_w, ln_b, pw_w, pw_b):
    val, gate = jnp.split(p, 2, axis=-1)
    h = val * jax.nn.sigmoid(gate)
    h = causal_dwconv(h, dw_w, dw_b)
    h = normalize_last(h, LN_EPS) * ln_w + ln_b
    h = jax.nn.silu(h)
    return h @ pw_w + pw_b


def rwkv7_scan(r, w, k, v, a, b):
    bsz, _, nh, n = r.shape

    def step(state, inp):
        r_t, w_t, k_t, v_t, a_t, b_t = inp
        sa = jnp.einsum('bhij,bhj->bhi', state, a_t)
        state = (state * w_t[:, :, None, :] + sa[..., None] * b_t[:, :, None, :]
                 + v_t[..., None] * k_t[:, :, None, :])
        y_t = jnp.einsum('bhij,bhj->bhi', state, r_t)
        return state, y_t

    xs = tuple(jnp.moveaxis(t.astype(jnp.float32), 1, 0) for t in (r, w, k, v, a, b))
    s0 = jnp.zeros((bsz, nh, n, n), jnp.float32)
    _, y = lax.scan(step, s0, xs)
    return jnp.moveaxis(y, 0, 1).astype(r.dtype)


def rwkv7_mixer(p, mu, w_up, w0, a_up, a0, g_up, k_k, k_a, r_k, lnx_w, lnx_b):
    bsz, s, _ = p.shape
    p = p + (token_shift(p) - p) * mu
    r, k, v, xw, xa, xg = jnp.split(
        p, [RWKV_DIM, 2 * RWKV_DIM, 3 * RWKV_DIM, 3 * RWKV_DIM + W_LORA, 3 * RWKV_DIM + W_LORA + A_LORA], axis=-1)
    w_log = -jax.nn.softplus(-(w0 + jnp.tanh(xw) @ w_up)) - 0.5
    decay = jnp.exp(-jnp.exp(w_log.astype(jnp.float32)))
    a = jax.nn.sigmoid(a0 + xa @ a_up)
    g = jax.nn.sigmoid(xg) @ g_up

    def heads(t):
        return t.reshape(bsz, s, RWKV_HEADS, HEAD_DIM)

    kk = heads(k * k_k).astype(jnp.float32)
    kk = (kk / jnp.maximum(jnp.sqrt(jnp.sum(kk * kk, axis=-1, keepdims=True)), 1e-12)).astype(k.dtype)
    k = k * (1 + (a - 1) * k_a)
    r_h, k_h, v_h, a_h = heads(r), heads(k), heads(v), heads(a)
    y = rwkv7_scan(r_h, heads(decay), k_h, v_h, -kk, kk * a_h)
    y = normalize_last(y, RWKV_LNX_EPS).reshape(bsz, s, RWKV_DIM) * lnx_w + lnx_b
    bonus = jnp.sum(r_h * k_h * r_k, axis=-1, keepdims=True) * v_h
    return (y + bonus.reshape(bsz, s, RWKV_DIM)) * g


def t5_causal_bucket(dist):
    max_exact = N_BUCKETS // 2
    dist = jnp.maximum(dist, 0)
    log_ratio = jnp.log(jnp.maximum(dist, 1).astype(jnp.float32) / max_exact) / math.log(MAX_DISTANCE / max_exact)
    large = max_exact + (log_ratio * (N_BUCKETS - max_exact)).astype(jnp.int32)
    return jnp.where(dist < max_exact, dist, jnp.minimum(large, N_BUCKETS - 1))


def sliding_window_attention(q, k, v, sinks, rel_bias):
    bsz, s = q.shape[:2]
    nb = s // BLOCK
    qb = q.reshape(bsz, nb, BLOCK, ATT_KV_HEADS, ATT_GROUP, HEAD_DIM)

    def with_prev(t):
        tb = t.reshape(bsz, nb, BLOCK, ATT_KV_HEADS, HEAD_DIM)
        prev = jnp.pad(tb, ((0, 0), (1, 0), (0, 0), (0, 0), (0, 0)))[:, :-1]
        return jnp.concatenate([prev, tb], axis=2)

    kb, vb = with_prev(k), with_prev(v)
    logits = jnp.einsum('bnqhgd,bnkhd->bnhgqk', qb, kb,
                        preferred_element_type=jnp.float32) * (HEAD_DIM ** -0.5)
    qi = jnp.arange(BLOCK)[:, None]
    kj = jnp.arange(2 * BLOCK)[None, :]
    dist = qi + BLOCK - kj
    in_window = (dist >= 0) & (dist < WINDOW)
    key_pos = jnp.arange(nb)[:, None, None] * BLOCK - BLOCK + kj[None]
    valid = in_window[None] & (key_pos >= 0)
    bias = rel_bias[t5_causal_bucket(dist)].astype(jnp.float32)
    bias = jnp.transpose(bias, (2, 0, 1)).reshape(ATT_KV_HEADS, ATT_GROUP, BLOCK, 2 * BLOCK)
    logits = jnp.where(valid[None, :, None, None], logits + bias, -jnp.inf)
    sink = sinks.astype(jnp.float32).reshape(ATT_KV_HEADS, ATT_GROUP, 1, 1)
    m = jnp.maximum(jnp.max(logits, axis=-1, keepdims=True), sink)
    p = jnp.exp(logits - m)
    weights = p / (jnp.sum(p, axis=-1, keepdims=True) + jnp.exp(sink - m))
    out = jnp.einsum('bnhgqk,bnkhd->bnqhgd', weights.astype(v.dtype), vb)
    return out.reshape(bsz, s, ATT_DIM)


def conv_glu_ffn(x, w_in, conv_w, conv_b, w_out):
    h = causal_dwconv(x @ w_in, conv_w, conv_b)
    gate, up = jnp.split(h, 2, axis=-1)
    return (jax.nn.silu(gate) * up) @ w_out


def setup_inputs(seed: int = 0) -> dict:
    key = jax.random.key(seed)
    ks = jax.random.split(key, 32)
    f32 = jnp.float32

    def nrm(k, shape, scale):
        return jax.random.normal(k, shape, f32) * scale

    def gain(k, shape):
        return 1.0 + 0.02 * jax.random.normal(k, shape, f32)

    L = DEPTH
    return {
        'x': nrm(ks[0], (BATCH, SEQ, D_MODEL), 1.0),
        'mix_norm_w': gain(ks[1], (L, D_MODEL)),
        'w_in': nrm(ks[2], (L, D_MODEL, IN_COLS), D_MODEL ** -0.5),
        'conv_dw_w': nrm(ks[3], (L, CONV_WIDTH, CONV_CH), CONV_WIDTH ** -0.5),
        'conv_dw_b': nrm(ks[4], (L, CONV_CH), 0.02),
        'conv_ln_w': gain(ks[5], (L, CONV_CH)),
        'conv_ln_b': nrm(ks[6], (L, CONV_CH), 0.02),
        'conv_pw_w': nrm(ks[7], (L, CONV_CH, CONV_CH), CONV_CH ** -0.5),
        'conv_pw_b': nrm(ks[8], (L, CONV_CH), 0.02),
        'rwkv_mu': jax.random.uniform(ks[9], (L, RWKV_COLS), f32),
        'rwkv_w_up': nrm(ks[10], (L, W_LORA, RWKV_DIM), 0.5 * W_LORA ** -0.5),
        'rwkv_w0': jax.random.uniform(ks[11], (L, RWKV_DIM), f32, -6.0, -1.0),
        'rwkv_a_up': nrm(ks[12], (L, A_LORA, RWKV_DIM), 0.5 * A_LORA ** -0.5),
        'rwkv_a0': nrm(ks[13], (L, RWKV_DIM), 0.1),
        'rwkv_g_up': nrm(ks[14], (L, G_LORA, RWKV_DIM), G_LORA ** -0.5),
        'rwkv_k_k': 0.85 + 0.05 * jax.random.normal(ks[15], (L, RWKV_DIM), f32),
        'rwkv_k_a': gain(ks[16], (L, RWKV_DIM)),
        'rwkv_r_k': nrm(ks[17], (L, RWKV_HEADS, HEAD_DIM), 0.1),
        'rwkv_lnx_w': gain(ks[18], (L, RWKV_DIM)),
        'rwkv_lnx_b': nrm(ks[19], (L, RWKV_DIM), 0.02),
        'attn_sinks': nrm(ks[20], (L, ATT_Q_HEADS), 1.0),
        'rel_bias': nrm(ks[21], (N_BUCKETS, ATT_Q_HEADS), 0.1),
        'w_out': nrm(ks[22], (L, MIX_DIM, D_MODEL), MIX_DIM ** -0.5),
        'ffn_norm_w': gain(ks[23], (L, D_MODEL)),
        'ffn_w_in': nrm(ks[24], (L, D_MODEL, 2 * D_FF), D_MODEL ** -0.5),
        'ffn_conv_w': nrm(ks[25], (L, FFN_CONV_WIDTH, 2 * D_FF), FFN_CONV_WIDTH ** -0.5),
        'ffn_conv_b': nrm(ks[26], (L, 2 * D_FF), 0.02),
        'ffn_w_out': nrm(ks[27], (L, D_FF, D_MODEL), D_FF ** -0.5),
        'final_norm_w': gain(ks[28], (D_MODEL,)),
    }


def reference(x, mix_norm_w, w_in, conv_dw_w, conv_dw_b, conv_ln_w, conv_ln_b, conv_pw_w, conv_pw_b,
              rwkv_mu, rwkv_w_up, rwkv_w0, rwkv_a_up, rwkv_a0, rwkv_g_up, rwkv_k_k, rwkv_k_a, rwkv_r_k,
              rwkv_lnx_w, rwkv_lnx_b, attn_sinks, rel_bias, w_out, ffn_norm_w, ffn_w_in, ffn_conv_w,
              ffn_conv_b, ffn_w_out, final_norm_w):
    bsz, s, _ = x.shape
    h = x
    for l in range(DEPTH):
        u = rmsnorm(h, mix_norm_w[l])
        proj = u @ w_in[l]
        p_conv, p_rwkv, p_att = jnp.split(proj, [2 * CONV_CH, 2 * CONV_CH + RWKV_COLS], axis=-1)
        y_conv = conformer_conv_mixer(p_conv, conv_dw_w[l], conv_dw_b[l], conv_ln_w[l], conv_ln_b[l],
                                      conv_pw_w[l], conv_pw_b[l])
        y_rwkv = rwkv7_mixer(p_rwkv, rwkv_mu[l], rwkv_w_up[l], rwkv_w0[l], rwkv_a_up[l], rwkv_a0[l],
                             rwkv_g_up[l], rwkv_k_k[l], rwkv_k_a[l], rwkv_r_k[l], rwkv_lnx_w[l], rwkv_lnx_b[l])
        q, k, v = jnp.split(p_att, [ATT_DIM, ATT_DIM + ATT_KV_DIM], axis=-1)
        y_att = sliding_window_attention(q.reshape(bsz, s, ATT_Q_HEADS, HEAD_DIM),
                                         k.reshape(bsz, s, ATT_KV_HEADS, HEAD_DIM),
                                         v.reshape(bsz, s, ATT_KV_HEADS, HEAD_DIM),
                                         attn_sinks[l], rel_bias)
        h = h + jnp.concatenate([y_conv, y_rwkv, y_att], axis=-1) @ w_out[l]
        h = h + conv_glu_ffn(rmsnorm(h, ffn_norm_w[l]), ffn_w_in[l], ffn_conv_w[l], ffn_conv_b[l], ffn_w_out[l])
    return rmsnorm(h, final_norm_w)
```

```python
import functools
import math

import jax
import jax.numpy as jnp
from jax import lax
from jax.experimental import pallas as pl
from jax.experimental.pallas import tpu as pltpu

F32 = jnp.float32
BF16 = jnp.bfloat16

HEAD_DIM = 64
CONV_WIDTH = 31
W_LORA = 64
A_LORA = 64
G_LORA = 160
RWKV_LNX_EPS = 64e-5
ATT_GROUP = 8
WINDOW = 128
BLOCK = 128
N_BUCKETS = 32
MAX_DISTANCE = 128
FFN_CONV_WIDTH = 3
RMS_EPS = 1e-6
LN_EPS = 1e-5

LANES = 128
LORA_PAD = 384
RWKV_CHUNK = 64
HEADS_PER_GROUP = 4
GROUP_LANES = HEADS_PER_GROUP * HEAD_DIM
CONV_HALO = 32
FFN_HALO = 8
VMEM_LIMIT = 56 * 1024 * 1024


def _cparams(sem):
    return pltpu.CompilerParams(dimension_semantics=sem, vmem_limit_bytes=VMEM_LIMIT)


def _dot(a, b):
    return jnp.dot(a, b, preferred_element_type=F32)


def _dot_nt(a, b):
    return lax.dot_general(a, b, (((1,), (1,)), ((), ())), preferred_element_type=F32)


def _split2(x):
    hi = x.astype(BF16)
    lo = (x - hi.astype(F32)).astype(BF16)
    return hi, lo


def _split3(x):
    hi = x.astype(BF16)
    r1 = x - hi.astype(F32)
    mid = r1.astype(BF16)
    lo = (r1 - mid.astype(F32)).astype(BF16)
    return hi, mid, lo


def _rmsnorm_body(x_ref, w_ref, o_ref):
    x = x_ref[...]
    y = x * lax.rsqrt(jnp.mean(x * x, axis=-1, keepdims=True) + RMS_EPS)
    o_ref[...] = (y * w_ref[...]).astype(o_ref.dtype)


def _rmsnorm(x, w, out_dtype, tm):
    s, d = x.shape
    return pl.pallas_call(
        _rmsnorm_body,
        grid=(s // tm,),
        in_specs=[pl.BlockSpec((tm, d), lambda i: (i, 0)),
                  pl.BlockSpec((1, d), lambda i: (0, 0))],
        out_specs=pl.BlockSpec((tm, d), lambda i: (i, 0)),
        out_shape=jax.ShapeDtypeStruct((s, d), out_dtype),
        compiler_params=_cparams(("parallel",)),
        name="rmsnorm",
    )(x, w.reshape(1, d))


def _mm_body(x_ref, w_ref, o_ref):
    o_ref[...] = _dot(x_ref[...], w_ref[...]).astype(o_ref.dtype)


def _matmul(x, w, out_dtype, tm, tn):
    m, k = x.shape
    n = w.shape[1]
    return pl.pallas_call(
        _mm_body,
        grid=(m // tm, n // tn),
        in_specs=[pl.BlockSpec((tm, k), lambda i, j: (i, 0)),
                  pl.BlockSpec((k, tn), lambda i, j: (0, j))],
        out_specs=pl.BlockSpec((tm, tn), lambda i, j: (i, j)),
        out_shape=jax.ShapeDtypeStruct((m, n), out_dtype),
        compiler_params=_cparams(("parallel", "arbitrary")),
        name="in_proj",
    )(x, w)


def _mix_out_body(yc_ref, yr_ref, ya_ref, wc_ref, wr_ref, wa_ref, h_ref, o_ref):
    acc = _dot(yc_ref[...], wc_ref[...])
    acc += _dot(yr_ref[...], wr_ref[...])
    acc += _dot(ya_ref[...], wa_ref[...])
    o_ref[...] = h_ref[...] + acc


def _mix_out(y_conv, y_rwkv, y_att, w_out, h, tm, tn):
    m, kc = y_conv.shape
    kr = y_rwkv.shape[1]
    ka = y_att.shape[1]
    assert kc == kr and ka == 2 * kc
    n = w_out.shape[1]
    return pl.pallas_call(
        _mix_out_body,
        grid=(m // tm, n // tn),
        in_specs=[pl.BlockSpec((tm, kc), lambda i, j: (i, 0)),
                  pl.BlockSpec((tm, kr), lambda i, j: (i, 0)),
                  pl.BlockSpec((tm, ka), lambda i, j: (i, 0)),
                  pl.BlockSpec((kc, tn), lambda i, j: (0, j)),
                  pl.BlockSpec((kr, tn), lambda i, j: (1, j)),
                  pl.BlockSpec((ka, tn), lambda i, j: (1, j)),
                  pl.BlockSpec((tm, tn), lambda i, j: (i, j))],
        out_specs=pl.BlockSpec((tm, tn), lambda i, j: (i, j)),
        out_shape=jax.ShapeDtypeStruct((m, n), F32),
        compiler_params=_cparams(("parallel", "arbitrary")),
        name="mix_out",
    )(y_conv, y_rwkv, y_att, w_out, w_out, w_out, h)


def _ffn_out_body(x_ref, w_ref, h_ref, o_ref, acc_ref, *, nk):
    k = pl.program_id(2)

    @pl.when(k == 0)
    def _():
        acc_ref[...] = h_ref[...]

    acc_ref[...] += _dot(x_ref[...], w_ref[...])

    @pl.when(k == nk - 1)
    def _():
        o_ref[...] = acc_ref[...]


def _ffn_out(g, w, h, tm, tn, tk):
    m, kdim = g.shape
    n = w.shape[1]
    nk = kdim // tk
    return pl.pallas_call(
        functools.partial(_ffn_out_body, nk=nk),
        grid=(m // tm, n // tn, nk),
        in_specs=[pl.BlockSpec((tm, tk), lambda i, j, k: (i, k)),
                  pl.BlockSpec((tk, tn), lambda i, j, k: (k, j)),
                  pl.BlockSpec((tm, tn), lambda i, j, k: (i, j))],
        out_specs=pl.BlockSpec((tm, tn), lambda i, j, k: (i, j)),
        out_shape=jax.ShapeDtypeStruct((m, n), F32),
        scratch_shapes=[pltpu.VMEM((tm, tn), F32)],
        compiler_params=_cparams(("parallel", "arbitrary", "arbitrary")),
        name="ffn_out",
    )(g, w, h)


def _ffn_in_body(u_ref, wg_ref, wu_ref, cwg_ref, cwu_ref, cbg_ref, cbu_ref, o_ref, hg_s, hu_s, *, tm):
    m = pl.program_id(1)

    @pl.when(m == 0)
    def _():
        hg_s[0:FFN_HALO, :] = jnp.zeros((FFN_HALO, hg_s.shape[1]), F32)
        hu_s[0:FFN_HALO, :] = jnp.zeros((FFN_HALO, hu_s.shape[1]), F32)

    u = u_ref[...]
    hg_s[FFN_HALO:FFN_HALO + tm, :] = _dot(u, wg_ref[...])
    hu_s[FFN_HALO:FFN_HALO + tm, :] = _dot(u, wu_ref[...])

    def conv(s, cw_ref, cb_ref):
        y = cw_ref[0:1, :] * s[FFN_HALO - 2:FFN_HALO - 2 + tm, :]
        y += cw_ref[1:2, :] * s[FFN_HALO - 1:FFN_HALO - 1 + tm, :]
        y += cw_ref[2:3, :] * s[FFN_HALO:FFN_HALO + tm, :]
        return y + cb_ref[...]

    gate = conv(hg_s, cwg_ref, cbg_ref)
    up = conv(hu_s, cwu_ref, cbu_ref)
    o_ref[...] = (gate * jax.nn.sigmoid(gate) * up).astype(o_ref.dtype)
    hg_s[0:FFN_HALO, :] = hg_s[tm:tm + FFN_HALO, :]
    hu_s[0:FFN_HALO, :] = hu_s[tm:tm + FFN_HALO, :]


def _ffn_in(u, w_in, conv_w, conv_b, dff, tm, tn):
    s, d = u.shape
    nj = dff // tn
    cw = jnp.pad(conv_w, ((0, 8 - FFN_CONV_WIDTH), (0, 0)))
    cb = conv_b.reshape(1, 2 * dff)
    return pl.pallas_call(
        functools.partial(_ffn_in_body, tm=tm),
        grid=(nj, s // tm),
        in_specs=[pl.BlockSpec((tm, d), lambda j, i: (i, 0)),
                  pl.BlockSpec((d, tn), lambda j, i: (0, j)),
                  pl.BlockSpec((d, tn), lambda j, i: (0, j + nj)),
                  pl.BlockSpec((8, tn), lambda j, i: (0, j)),
                  pl.BlockSpec((8, tn), lambda j, i: (0, j + nj)),
                  pl.BlockSpec((1, tn), lambda j, i: (0, j)),
                  pl.BlockSpec((1, tn), lambda j, i: (0, j + nj))],
        out_specs=pl.BlockSpec((tm, tn), lambda j, i: (i, j)),
        out_shape=jax.ShapeDtypeStruct((s, dff), BF16),
        scratch_shapes=[pltpu.VMEM((tm + FFN_HALO, tn), F32), pltpu.VMEM((tm + FFN_HALO, tn), F32)],
        compiler_params=_cparams(("parallel", "arbitrary")),
        name="ffn_in",
    )(u, w_in, w_in, cw, cw, cb, cb)


def _conformer_body(val_ref, gate_ref, valh_ref, gateh_ref, dww_ref, dwb_ref, lnw_ref, lnb_ref,
                    pww_ref, pwb_ref, o_ref, hs, ys, *, tm, ch):
    i = pl.program_id(0)
    halo = valh_ref[...].astype(F32) * jax.nn.sigmoid(gateh_ref[...].astype(F32))
    hs[0:CONV_HALO, :] = jnp.where(i > 0, halo, 0.0)
    hs[CONV_HALO:CONV_HALO + tm, :] = val_ref[...].astype(F32) * jax.nn.sigmoid(gate_ref[...].astype(F32))

    rs, ls = 32, 512
    base = CONV_HALO - (CONV_WIDTH - 1)
    for r0 in range(0, tm, rs):
        for l0 in range(0, ch, ls):
            acc = jnp.broadcast_to(dwb_ref[0:1, l0:l0 + ls], (rs, ls))
            for j in range(CONV_WIDTH):
                acc = acc + dww_ref[j:j + 1, l0:l0 + ls] * hs[r0 + base + j:r0 + base + j + rs, l0:l0 + ls]
            ys[r0:r0 + rs, l0:l0 + ls] = acc

    y = ys[...]
    mu = jnp.mean(y, axis=-1, keepdims=True)
    yc = y - mu
    var = jnp.mean(yc * yc, axis=-1, keepdims=True)
    z = (yc * lax.rsqrt(var + LN_EPS)) * lnw_ref[...] + lnb_ref[...]
    z = z * jax.nn.sigmoid(z)
    o_ref[...] = (_dot(z.astype(BF16), pww_ref[...]) + pwb_ref[...]).astype(o_ref.dtype)


def _conformer(proj, val_blk, gate_blk, dw_w, dw_b, ln_w, ln_b, pw_w, pw_b, tm):
    s = proj.shape[0]
    ch = dw_w.shape[1]
    hb = tm // CONV_HALO
    row = lambda v: v.reshape(1, ch)
    return pl.pallas_call(
        functools.partial(_conformer_body, tm=tm, ch=ch),
        grid=(s // tm,),
        in_specs=[pl.BlockSpec((tm, ch), lambda i: (i, val_blk)),
                  pl.BlockSpec((tm, ch), lambda i: (i, gate_blk)),
                  pl.BlockSpec((CONV_HALO, ch), lambda i: (jnp.maximum(i * hb - 1, 0), val_blk)),
                  pl.BlockSpec((CONV_HALO, ch), lambda i: (jnp.maximum(i * hb - 1, 0), gate_blk)),
                  pl.BlockSpec((32, ch), lambda i: (0, 0)),
                  pl.BlockSpec((1, ch), lambda i: (0, 0)),
                  pl.BlockSpec((1, ch), lambda i: (0, 0)),
                  pl.BlockSpec((1, ch), lambda i: (0, 0)),
                  pl.BlockSpec((ch, ch), lambda i: (0, 0)),
                  pl.BlockSpec((1, ch), lambda i: (0, 0))],
        out_specs=pl.BlockSpec((tm, ch), lambda i: (i, 0)),
        out_shape=jax.ShapeDtypeStruct((s, ch), BF16),
        scratch_shapes=[pltpu.VMEM((tm + CONV_HALO, ch), F32), pltpu.VMEM((tm, ch), F32)],
        compiler_params=_cparams(("parallel",)),
        name="conformer",
    )(proj, proj, proj, proj, jnp.pad(dw_w, ((0, 32 - CONV_WIDTH), (0, 0))), row(dw_b), row(ln_w), row(ln_b),
      pw_w.astype(BF16), row(pw_b))


def _rwkv_body(r_ref, k_ref, v_ref, l_ref, rh_ref, kh_ref, vh_ref, lh_ref,
               mur_ref, muk_ref, muv_ref, mul_ref, wup_ref, aup_ref, gup_ref,
               w0_ref, a0_ref, kk_ref, ka_ref, rk_ref, lnw_ref, lnb_ref,
               e1_ref, e2_ref, tri_ref, o_ref,
               sh, shl, rt_s, at_s, kt_s, bt_s, kw_s, bw_s, v_s, dc_s, y_s, g_s, *, tm, dim):
    i = pl.program_id(0)
    nc = tm // RWKV_CHUNK
    c = RWKV_CHUNK

    @pl.when(i == 0)
    def _():
        g_s[...] = jnp.zeros(g_s.shape, F32)

    def shifted_mix(x_ref, xh_ref, mu_ref, buf):
        x = x_ref[...].astype(F32)
        last = xh_ref[15:16, :].astype(F32)
        buf[7:8, :] = jnp.where(i > 0, last, 0.0)
        buf[8:8 + tm, :] = x
        prev = buf[7:7 + tm, :]
        return x + (prev - x) * mu_ref[...]

    def headsum(x):
        hi, lo = _split2(x)
        s = _dot(hi, e1_ref[...]) + _dot(lo, e1_ref[...])
        shi, slo = _split2(s)
        return _dot(shi, e2_ref[...]) + _dot(slo, e2_ref[...])

    r = shifted_mix(r_ref, rh_ref, mur_ref, sh)
    k = shifted_mix(k_ref, kh_ref, muk_ref, sh)
    v = shifted_mix(v_ref, vh_ref, muv_ref, sh)
    lo_ = shifted_mix(l_ref, lh_ref, mul_ref, shl)

    dw = _dot(jnp.tanh(lo_).astype(BF16), wup_ref[...])
    da = _dot(lo_.astype(BF16), aup_ref[...])
    g = _dot(jax.nn.sigmoid(lo_).astype(BF16), gup_ref[...])
    z = -(w0_ref[...] + dw)
    softplus = jnp.maximum(z, 0.0) + jnp.log(1.0 + jnp.exp(-jnp.abs(z)))
    logw = -jnp.exp(-softplus - 0.5)
    a = jax.nn.sigmoid(a0_ref[...] + da)
    kkr = k * kk_ref[...]
    kk = kkr / jnp.maximum(jnp.sqrt(headsum(kkr * kkr)), 1e-12)
    k2 = k * (1.0 + (a - 1.0) * ka_ref[...])
    bonus = headsum(r * k2 * rk_ref[...]) * v
    a_s = -kk
    b_s = kk * a
    v_s[...] = v

    tri = tri_ref[...]
    for ci in range(nc):
        sl = slice(ci * c, (ci + 1) * c)
        lw = logw[sl]
        h3, m3, l3 = _split3(lw)
        li = _dot(tri, h3) + _dot(tri, m3) + _dot(tri, l3)
        lc = li[c - 1:c, :]
        e_neg = jnp.exp(-li)
        e_end = jnp.exp(lc - li)
        rt_s[sl, :] = r[sl] * jnp.exp(li)
        at_s[sl, :] = a_s[sl] * jnp.exp(li - lw)
        kt_s[sl, :] = k2[sl] * e_neg
        bt_s[sl, :] = b_s[sl] * e_neg
        kw_s[sl, :] = k2[sl] * e_end
        bw_s[sl, :] = b_s[sl] * e_end
        dc_s[ci * 8:(ci + 1) * 8, :] = jnp.broadcast_to(jnp.exp(lc), (8, dim))

    gl = GROUP_LANES
    t_idx = lax.broadcasted_iota(jnp.int32, (c, gl), 0)
    s_idx = jnp.bitwise_and(lax.broadcasted_iota(jnp.int32, (c, gl), 1), HEAD_DIM - 1)
    strict = t_idx > s_idx
    incl = t_idx >= s_idx
    eye = (t_idx == s_idx).astype(F32)
    head_shift = HEAD_DIM.bit_length() - 1
    bd_mask = (jnp.right_shift(lax.broadcasted_iota(jnp.int32, (gl, gl), 0), head_shift)
               == jnp.right_shift(lax.broadcasted_iota(jnp.int32, (gl, gl), 1), head_shift))

    def bd(x):
        return jnp.where(bd_mask, jnp.concatenate([x] * HEADS_PER_GROUP, axis=0), 0.0).astype(BF16)

    def fold(x):
        xm = jnp.where(bd_mask, x, 0.0)
        out = xm[0:HEAD_DIM]
        for q in range(1, HEADS_PER_GROUP):
            out = out + xm[q * HEAD_DIM:(q + 1) * HEAD_DIM]
        return out

    def chunk_step(ci, carry):
        r0 = pl.multiple_of(ci * c, c)
        rows = pl.ds(r0, c)
        for q in range(dim // gl):
            ln = slice(q * gl, (q + 1) * gl)
            ag = at_s[rows, ln]
            rg = rt_s[rows, ln]
            kg = kt_s[rows, ln]
            bg = bt_s[rows, ln]
            vg = v_s[rows, ln]
            lhs = jnp.concatenate([ag, rg], axis=0).astype(BF16)
            pb = _dot_nt(lhs, bd(bg))
            pk = _dot_nt(lhs, bd(kg))
            n_ab = jnp.where(strict, pb[0:c], 0.0)
            a_rb = jnp.where(incl, pb[c:2 * c], 0.0)
            a_ak = jnp.where(strict, pk[0:c], 0.0)
            a_rk = jnp.where(incl, pk[c:2 * c], 0.0)
            p = eye + n_ab
            mpow = n_ab
            for _ in range(5):
                mpow = _dot(mpow.astype(BF16), bd(mpow))
                p = p + _dot(p.astype(BF16), bd(mpow))
            gg = g_s[:, ln]
            bd_g = bd(gg)
            bd_v = bd(vg)
            x1 = _dot_nt(ag.astype(BF16), bd_g) + _dot(a_ak.astype(BF16), bd_v)
            u = _dot(p.astype(BF16), bd(x1))
            y = (_dot_nt(rg.astype(BF16), bd_g) + _dot(a_rb.astype(BF16), bd(u))
                 + _dot(a_rk.astype(BF16), bd_v))
            y_s[rows, ln] = y
            uv = jnp.concatenate([u, vg], axis=0)
            bk = jnp.concatenate([bw_s[rows, ln], kw_s[rows, ln]], axis=0)
            upd = _dot(uv.T.astype(BF16), bk.astype(BF16))
            dc = dc_s[pl.ds(pl.multiple_of(ci * 8, 8), 8), ln][0:1]
            g_s[:, ln] = dc * gg + fold(upd)
        return carry

    lax.fori_loop(0, nc, chunk_step, 0)

    y = y_s[...]
    inv_n = 1.0 / HEAD_DIM
    mu = headsum(y) * inv_n
    yc = y - mu
    var = headsum(yc * yc) * inv_n
    yn = (yc * lax.rsqrt(var + RWKV_LNX_EPS)) * lnw_ref[...] + lnb_ref[...]
    o_ref[...] = ((yn + bonus) * g).astype(o_ref.dtype)


def _rwkv(proj, blk_r, blk_k, blk_v, blk_l, mu, w_up, w0, a_up, a0, g_up, k_k, k_a, r_k, lnx_w, lnx_b, tm):
    s = proj.shape[0]
    dim = w0.shape[0]
    nh = dim // HEAD_DIM
    hb = tm // 16
    row = lambda x: x.reshape(1, -1)
    mu_r, mu_k, mu_v = row(mu[0:dim]), row(mu[dim:2 * dim]), row(mu[2 * dim:3 * dim])
    nl = W_LORA + A_LORA + G_LORA
    mu_l = jnp.pad(row(mu[3 * dim:]), ((0, 0), (0, LORA_PAD - nl)))
    wup = jnp.pad(w_up, ((0, LORA_PAD - W_LORA), (0, 0))).astype(BF16)
    aup = jnp.pad(a_up, ((W_LORA, LORA_PAD - W_LORA - A_LORA), (0, 0))).astype(BF16)
    gup = jnp.pad(g_up, ((W_LORA + A_LORA, LORA_PAD - nl), (0, 0))).astype(BF16)
    head_of_lane = jnp.arange(dim) // HEAD_DIM
    e1 = (head_of_lane[:, None] == jnp.arange(LANES)[None, :]).astype(BF16)
    e2 = e1.T
    tri = jnp.tril(jnp.ones((RWKV_CHUNK, RWKV_CHUNK), BF16))
    nc = tm // RWKV_CHUNK

    def tile(blk, w):
        return pl.BlockSpec((tm, w), lambda i: (i, blk))

    def halo(blk, w):
        return pl.BlockSpec((16, w), lambda i: (jnp.maximum(i * hb - 1, 0), blk))

    def const(shape):
        return pl.BlockSpec(shape, lambda i: (0, 0))

    big = lambda: pltpu.VMEM((tm, dim), F32)
    return pl.pallas_call(
        functools.partial(_rwkv_body, tm=tm, dim=dim),
        grid=(s // tm,),
        in_specs=[tile(blk_r, dim), tile(blk_k, dim), tile(blk_v, dim), tile(blk_l, LORA_PAD),
                  halo(blk_r, dim), halo(blk_k, dim), halo(blk_v, dim), halo(blk_l, LORA_PAD),
                  const((1, dim)), const((1, dim)), const((1, dim)), const((1, LORA_PAD)),
                  const((LORA_PAD, dim)), const((LORA_PAD, dim)), const((LORA_PAD, dim)),
                  const((1, dim)), const((1, dim)), const((1, dim)), const((1, dim)), const((1, dim)),
                  const((1, dim)), const((1, dim)),
                  const((dim, LANES)), const((LANES, dim)), const((RWKV_CHUNK, RWKV_CHUNK))],
        out_specs=pl.BlockSpec((tm, dim), lambda i: (i, 0)),
        out_shape=jax.ShapeDtypeStruct((s, dim), BF16),
        scratch_shapes=[pltpu.VMEM((tm + 8, dim), F32), pltpu.VMEM((tm + 8, LORA_PAD), F32),
                        big(), big(), big(), big(), big(), big(), big(),
                        pltpu.VMEM((nc * 8, dim), F32), big(),
                        pltpu.VMEM((HEAD_DIM, dim), F32)],
        compiler_params=_cparams(("arbitrary",)),
        name="rwkv7",
    )(proj, proj, proj, proj, proj, proj, proj, proj,
      mu_r, mu_k, mu_v, mu_l, wup, aup, gup,
      row(w0), row(a0), row(k_k), row(k_a), row(r_k), row(lnx_w), row(lnx_b), e1, e2, tri)


def _t5_bucket_table():
    max_exact = N_BUCKETS // 2
    qi = jnp.arange(BLOCK)[:, None]
    kj = jnp.arange(2 * BLOCK)[None, :]
    dist = qi + BLOCK - kj
    in_window = (dist >= 0) & (dist < WINDOW)
    d = jnp.maximum(dist, 0)
    log_ratio = jnp.log(jnp.maximum(d, 1).astype(F32) / max_exact) / math.log(MAX_DISTANCE / max_exact)
    large = max_exact + (log_ratio * (N_BUCKETS - max_exact)).astype(jnp.int32)
    bucket = jnp.where(d < max_exact, d, jnp.minimum(large, N_BUCKETS - 1))
    return jnp.where(in_window, bucket, -1).astype(jnp.int32)


def _bias_body(bucket_ref, rb_ref, o_ref):
    h = pl.program_id(0)
    bucket = bucket_ref[...]
    acc = jnp.zeros(bucket.shape, F32)
    for b in range(N_BUCKETS):
        acc = jnp.where(bucket == b, rb_ref[b, h], acc)
    o_ref[0] = jnp.where(bucket < 0, -jnp.inf, acc)


def _attn_bias(rel_bias):
    nh = rel_bias.shape[1]
    return pl.pallas_call(
        _bias_body,
        grid=(nh,),
        in_specs=[pl.BlockSpec((BLOCK, 2 * BLOCK), lambda h: (0, 0)),
                  pl.BlockSpec(memory_space=pltpu.SMEM)],
        out_specs=pl.BlockSpec((1, BLOCK, 2 * BLOCK), lambda h: (h, 0, 0)),
        out_shape=jax.ShapeDtypeStruct((nh, BLOCK, 2 * BLOCK), F32),
        compiler_params=_cparams(("arbitrary",)),
        name="attn_bias",
    )(_t5_bucket_table(), rel_bias)


def _attn_body(q_ref, k_ref, v_ref, kh_ref, vh_ref, bias_ref, sink_ref, o_ref, kb, vb, kbs, vbs, *, tq, nq, nkv):
    i = pl.program_id(0)
    nblk = tq // BLOCK
    kvw = nkv * HEAD_DIM
    kb[0:BLOCK, :] = kh_ref[...]
    kb[BLOCK:BLOCK + tq, :] = k_ref[...]
    vb[0:BLOCK, :] = vh_ref[...]
    vb[BLOCK:BLOCK + tq, :] = v_ref[...]
    kbs[...] = pltpu.roll(kb[...].astype(F32), HEAD_DIM, axis=1).astype(BF16)
    vbs[...] = pltpu.roll(vb[...].astype(F32), HEAD_DIM, axis=1).astype(BF16)

    lane = lax.broadcasted_iota(jnp.int32, (BLOCK, LANES), 1)
    low_half = lane < HEAD_DIM
    key_in_own_block = lax.broadcasted_iota(jnp.int32, (BLOCK, 2 * BLOCK), 1) >= BLOCK
    scale = HEAD_DIM ** -0.5

    def block_step(b, carry):
        r0 = pl.multiple_of(b * BLOCK, BLOCK)
        has_prev = (i * nblk + b) > 0
        key_ok = jnp.logical_or(has_prev, key_in_own_block)
        kwin = pl.ds(r0, 2 * BLOCK)
        for pair in range(nq // 2):
            g = (2 * pair) // ATT_GROUP
            q2 = q_ref[pl.ds(r0, BLOCK), pair * LANES:(pair + 1) * LANES]
            halves = []
            for par in range(2):
                h = 2 * pair + par
                if (g % 2) == par:
                    kk, vv, pos = kb, vb, g
                else:
                    kk, vv, pos = kbs, vbs, (g + 1) % nkv
                kv_ln = slice((pos // 2) * LANES, (pos // 2 + 1) * LANES)
                qm = jnp.where(low_half if par == 0 else jnp.logical_not(low_half), q2, jnp.zeros_like(q2))
                logits = _dot_nt(qm, kk[kwin, kv_ln]) * scale + bias_ref[h]
                logits = jnp.where(key_ok, logits, -jnp.inf)
                sink = sink_ref[h]
                mx = jnp.maximum(jnp.max(logits, axis=-1, keepdims=True), sink)
                p = jnp.exp(logits - mx)
                denom = jnp.sum(p, axis=-1, keepdims=True) + jnp.exp(sink - mx)
                halves.append(_dot(p.astype(BF16), vv[kwin, kv_ln]) / denom)
            o_ref[pl.ds(r0, BLOCK), pair * LANES:(pair + 1) * LANES] = (
                jnp.where(low_half, halves[0], halves[1]).astype(o_ref.dtype))
        return carry

    lax.fori_loop(0, nblk, block_step, 0)


def _attention(proj, q_blk, k_blk, v_blk, nq, nkv, bias, sinks, tq):
    s = proj.shape[0]
    qw = nq * HEAD_DIM
    kvw = nkv * HEAD_DIM
    hb = tq // BLOCK
    return pl.pallas_call(
        functools.partial(_attn_body, tq=tq, nq=nq, nkv=nkv),
        grid=(s // tq,),
        in_specs=[pl.BlockSpec((tq, qw), lambda i: (i, q_blk)),
                  pl.BlockSpec((tq, kvw), lambda i: (i, k_blk)),
                  pl.BlockSpec((tq, kvw), lambda i: (i, v_blk)),
                  pl.BlockSpec((BLOCK, kvw), lambda i: (jnp.maximum(i * hb - 1, 0), k_blk)),
                  pl.BlockSpec((BLOCK, kvw), lambda i: (jnp.maximum(i * hb - 1, 0), v_blk)),
                  pl.BlockSpec((nq, BLOCK, 2 * BLOCK), lambda i: (0, 0, 0)),
                  pl.BlockSpec(memory_space=pltpu.SMEM)],
        out_specs=pl.BlockSpec((tq, qw), lambda i: (i, 0)),
        out_shape=jax.ShapeDtypeStruct((s, qw), BF16),
        scratch_shapes=[pltpu.VMEM((tq + BLOCK, kvw), BF16), pltpu.VMEM((tq + BLOCK, kvw), BF16),
                        pltpu.VMEM((tq + BLOCK, kvw), BF16), pltpu.VMEM((tq + BLOCK, kvw), BF16)],
        compiler_params=_cparams(("parallel",)),
        name="swa",
    )(proj, proj, proj, proj, proj, bias, sinks)


def _tile(n, pref):
    t = min(n, pref)
    assert n % t == 0
    return t


def kernel(x, mix_norm_w, w_in, conv_dw_w, conv_dw_b, conv_ln_w, conv_ln_b, conv_pw_w, conv_pw_b, rwkv_mu, rwkv_w_up, rwkv_w0, rwkv_a_up, rwkv_a0, rwkv_g_up, rwkv_k_k, rwkv_k_a, rwkv_r_k, rwkv_lnx_w, rwkv_lnx_b, attn_sinks, rel_bias, w_out, ffn_norm_w, ffn_w_in, ffn_conv_w, ffn_conv_b, ffn_w_out, final_norm_w):
    bsz, s, d = x.shape
    assert bsz == 1
    depth = w_in.shape[0]
    cc = conv_dw_w.shape[2]
    rd = rwkv_w0.shape[1]
    nq = attn_sinks.shape[1]
    nkv = nq // ATT_GROUP
    ad, akv = nq * HEAD_DIM, nkv * HEAD_DIM
    nl = W_LORA + A_LORA + G_LORA
    dff = ffn_w_out.shape[1]
    assert cc == rd and ad == 2 * cc and 4 * akv == cc and w_in.shape[2] == 2 * cc + 3 * rd + nl + ad + 2 * akv
    assert s % 512 == 0

    o_conv, o_rwkv, o_att = 0, 2 * cc, 2 * cc + 3 * rd + nl
    n_proj = ad + 2 * cc + 3 * rd + 2 * akv + LORA_PAD
    n_proj_pad = -(-n_proj // 1024) * 1024
    col_q = 0
    col_val, col_gate = ad // cc, ad // cc + 1
    col_r, col_k, col_v = ad // cc + 2, ad // cc + 3, ad // cc + 4
    off_kv = ad + 2 * cc + 3 * rd
    col_ak, col_av = off_kv // akv, off_kv // akv + 1
    off_l = off_kv + 2 * akv
    assert off_l % LORA_PAD == 0
    col_l = off_l // LORA_PAD

    dff_pad = -(-dff // 1024) * 1024

    def prep_w_in(w):
        return jnp.concatenate(
            [w[:, o_att:o_att + ad], w[:, o_conv:o_conv + 2 * cc], w[:, o_rwkv:o_rwkv + 3 * rd],
             w[:, o_att + ad:o_att + ad + 2 * akv], w[:, o_rwkv + 3 * rd:o_rwkv + 3 * rd + nl],
             jnp.zeros((d, n_proj_pad - n_proj + LORA_PAD - nl), w.dtype)], axis=1).astype(BF16)

    def pad_ff(a, axis):
        g_, u_ = jnp.split(a, 2, axis=axis)
        padw = [(0, 0)] * a.ndim
        padw[axis] = (0, dff_pad - dff)
        return jnp.concatenate([jnp.pad(g_, padw), jnp.pad(u_, padw)], axis=axis)

    tm = _tile(s, 1024)
    bias = _attn_bias(rel_bias)
    h = x.reshape(s, d)
    for l in range(depth):
        u = _rmsnorm(h, mix_norm_w[l], BF16, _tile(s, 512))
        proj = _matmul(u, prep_w_in(w_in[l]), BF16, tm, 1024)
        y_conv = _conformer(proj, col_val, col_gate, conv_dw_w[l], conv_dw_b[l], conv_ln_w[l], conv_ln_b[l],
                            conv_pw_w[l], conv_pw_b[l], _tile(s, 256))
        y_rwkv = _rwkv(proj, col_r, col_k, col_v, col_l, rwkv_mu[l], rwkv_w_up[l], rwkv_w0[l], rwkv_a_up[l],
                       rwkv_a0[l], rwkv_g_up[l], rwkv_k_k[l], rwkv_k_a[l], rwkv_r_k[l].reshape(-1),
                       rwkv_lnx_w[l], rwkv_lnx_b[l], _tile(s, 256))
        y_att = _attention(proj, col_q, col_ak, col_av, nq, nkv, bias, attn_sinks[l], _tile(s, 512))
        h = _mix_out(y_conv, y_rwkv, y_att, w_out[l].astype(BF16), h, tm, 512)
        u = _rmsnorm(h, ffn_norm_w[l], BF16, _tile(s, 512))
        gact = _ffn_in(u, pad_ff(ffn_w_in[l], 1).astype(BF16), pad_ff(ffn_conv_w[l], 1), pad_ff(ffn_conv_b[l], 0),
                       dff_pad, tm, 512)
        w2 = jnp.pad(ffn_w_out[l], ((0, dff_pad - dff), (0, 0))).astype(BF16)
        h = _ffn_out(gact, w2, h, tm, 1024, dff_pad // 4)
    out = _rmsnorm(h, final_norm_w, F32, _tile(s, 512))
    return out.reshape(bsz, s, d)
```

```python
import functools
import math

import jax
import jax.numpy as jnp
from jax import lax
from jax.experimental import pallas as pl
from jax.experimental.pallas import tpu as pltpu

F32 = jnp.float32
BF16 = jnp.bfloat16

HEAD_DIM = 64
CONV_WIDTH = 31
W_LORA = 64
A_LORA = 64
G_LORA = 160
RWKV_LNX_EPS = 64e-5
ATT_GROUP = 8
WINDOW = 128
BLOCK = 128
N_BUCKETS = 32
MAX_DISTANCE = 128
FFN_CONV_WIDTH = 3
RMS_EPS = 1e-6
LN_EPS = 1e-5

LANES = 128
LORA_PAD = 384
RWKV_CHUNK = 64
HEADS_PER_GROUP = 4
GROUP_LANES = HEADS_PER_GROUP * HEAD_DIM
CONV_HALO = 32
FFN_HALO = 8
VMEM_LIMIT = 56 * 1024 * 1024


def _cparams(sem):
    return pltpu.CompilerParams(dimension_semantics=sem, vmem_limit_bytes=VMEM_LIMIT)


def _dot(a, b):
    return jnp.dot(a, b, preferred_element_type=F32)


def _dot_nt(a, b):
    return lax.dot_general(a, b, (((1,), (1,)), ((), ())), preferred_element_type=F32)


def _split2(x):
    hi = x.astype(BF16)
    lo = (x - hi.astype(F32)).astype(BF16)
    return hi, lo


def _split3(x):
    hi = x.astype(BF16)
    r1 = x - hi.astype(F32)
    mid = r1.astype(BF16)
    lo = (r1 - mid.astype(F32)).astype(BF16)
    return hi, mid, lo


def _rmsnorm_body(x_ref, w_ref, o_ref):
    x = x_ref[...]
    y = x * lax.rsqrt(jnp.mean(x * x, axis=-1, keepdims=True) + RMS_EPS)
    o_ref[...] = (y * w_ref[...]).astype(o_ref.dtype)


def _rmsnorm(x, w, out_dtype, tm):
    s, d = x.shape
    return pl.pallas_call(
        _rmsnorm_body,
        grid=(s // tm,),
        in_specs=[pl.BlockSpec((tm, d), lambda i: (i, 0)),
                  pl.BlockSpec((1, d), lambda i: (0, 0))],
        out_specs=pl.BlockSpec((tm, d), lambda i: (i, 0)),
        out_shape=jax.ShapeDtypeStruct((s, d), out_dtype),
        compiler_params=_cparams(("parallel",)),
        name="rmsnorm",
    )(x, w.reshape(1, d))


def _mm_body(x_ref, w_ref, o_ref):
    o_ref[...] = _dot(x_ref[...], w_ref[...]).astype(o_ref.dtype)


def _matmul(x, w, layer, out_dtype, tm, tn):
    m, k = x.shape
    n = w.shape[2]
    return pl.pallas_call(
        _mm_body,
        grid=(m // tm, n // tn),
        in_specs=[pl.BlockSpec((tm, k), lambda i, j: (i, 0)),
                  pl.BlockSpec((None, k, tn), lambda i, j: (layer, 0, j))],
        out_specs=pl.BlockSpec((tm, tn), lambda i, j: (i, j)),
        out_shape=jax.ShapeDtypeStruct((m, n), out_dtype),
        compiler_params=_cparams(("parallel", "arbitrary")),
        name="in_proj",
    )(x, w)


def _mix_out_body(yc_ref, yr_ref, ya_ref, wc_ref, wr_ref, wa_ref, h_ref, o_ref):
    acc = _dot(yc_ref[...], wc_ref[...])
    acc += _dot(yr_ref[...], wr_ref[...])
    acc += _dot(ya_ref[...], wa_ref[...])
    o_ref[...] = h_ref[...] + acc


def _mix_out(y_conv, y_rwkv, y_att, w_out, layer, h, tm, tn):
    m, kc = y_conv.shape
    kr = y_rwkv.shape[1]
    ka = y_att.shape[1]
    assert kc == kr and ka == 2 * kc
    n = w_out.shape[2]
    return pl.pallas_call(
        _mix_out_body,
        grid=(m // tm, n // tn),
        in_specs=[pl.BlockSpec((tm, kc), lambda i, j: (i, 0)),
                  pl.BlockSpec((tm, kr), lambda i, j: (i, 0)),
                  pl.BlockSpec((tm, ka), lambda i, j: (i, 0)),
                  pl.BlockSpec((None, kc, tn), lambda i, j: (layer, 0, j)),
                  pl.BlockSpec((None, kr, tn), lambda i, j: (layer, 1, j)),
                  pl.BlockSpec((None, ka, tn), lambda i, j: (layer, 1, j)),
                  pl.BlockSpec((tm, tn), lambda i, j: (i, j))],
        out_specs=pl.BlockSpec((tm, tn), lambda i, j: (i, j)),
        out_shape=jax.ShapeDtypeStruct((m, n), F32),
        compiler_params=_cparams(("parallel", "arbitrary")),
        name="mix_out",
    )(y_conv, y_rwkv, y_att, w_out, w_out, w_out, h)


def _ffn_out_body(x_ref, w_ref, h_ref, o_ref, acc_ref, *, nk):
    k = pl.program_id(2)

    @pl.when(k == 0)
    def _():
        acc_ref[...] = h_ref[...]

    acc_ref[...] += _dot(x_ref[...], w_ref[...])

    @pl.when(k == nk - 1)
    def _():
        o_ref[...] = acc_ref[...]


def _ffn_out(g, w, layer, h, tm, tn, tk):
    m, kdim = g.shape
    n = w.shape[2]
    nk = kdim // tk
    return pl.pallas_call(
        functools.partial(_ffn_out_body, nk=nk),
        grid=(m // tm, n // tn, nk),
        in_specs=[pl.BlockSpec((tm, tk), lambda i, j, k: (i, k)),
                  pl.BlockSpec((None, tk, tn), lambda i, j, k: (layer, k, j)),
                  pl.BlockSpec((tm, tn), lambda i, j, k: (i, j))],
        out_specs=pl.BlockSpec((tm, tn), lambda i, j, k: (i, j)),
        out_shape=jax.ShapeDtypeStruct((m, n), F32),
        scratch_shapes=[pltpu.VMEM((tm, tn), F32)],
        compiler_params=_cparams(("parallel", "arbitrary", "arbitrary")),
        name="ffn_out",
    )(g, w, h)


def _ffn_in_body(u_ref, wg_ref, wu_ref, cwg_ref, cwu_ref, cbg_ref, cbu_ref, o_ref, hg_s, hu_s, *, tm):
    m = pl.program_id(1)

    @pl.when(m == 0)
    def _():
        hg_s[0:FFN_HALO, :] = jnp.zeros((FFN_HALO, hg_s.shape[1]), F32)
        hu_s[0:FFN_HALO, :] = jnp.zeros((FFN_HALO, hu_s.shape[1]), F32)

    u = u_ref[...]
    hg_s[FFN_HALO:FFN_HALO + tm, :] = _dot(u, wg_ref[...])
    hu_s[FFN_HALO:FFN_HALO + tm, :] = _dot(u, wu_ref[...])

    def conv(s, cw_ref, cb_ref):
        y = cw_ref[0:1, :] * s[FFN_HALO - 2:FFN_HALO - 2 + tm, :]
        y += cw_ref[1:2, :] * s[FFN_HALO - 1:FFN_HALO - 1 + tm, :]
        y += cw_ref[2:3, :] * s[FFN_HALO:FFN_HALO + tm, :]
        return y + cb_ref[...]

    gate = conv(hg_s, cwg_ref, cbg_ref)
    up = conv(hu_s, cwu_ref, cbu_ref)
    o_ref[...] = (gate * jax.nn.sigmoid(gate) * up).astype(o_ref.dtype)
    hg_s[0:FFN_HALO, :] = hg_s[tm:tm + FFN_HALO, :]
    hu_s[0:FFN_HALO, :] = hu_s[tm:tm + FFN_HALO, :]


def _ffn_in(u, w_in, layer, conv_w, conv_b, dff, tm, tn):
    s, d = u.shape
    nj = dff // tn
    cw = jnp.pad(conv_w, ((0, 8 - FFN_CONV_WIDTH), (0, 0)))
    cb = conv_b.reshape(1, 2 * dff)
    return pl.pallas_call(
        functools.partial(_ffn_in_body, tm=tm),
        grid=(nj, s // tm),
        in_specs=[pl.BlockSpec((tm, d), lambda j, i: (i, 0)),
                  pl.BlockSpec((None, d, tn), lambda j, i: (layer, 0, j)),
                  pl.BlockSpec((None, d, tn), lambda j, i: (layer, 0, j + nj)),
                  pl.BlockSpec((8, tn), lambda j, i: (0, j)),
                  pl.BlockSpec((8, tn), lambda j, i: (0, j + nj)),
                  pl.BlockSpec((1, tn), lambda j, i: (0, j)),
                  pl.BlockSpec((1, tn), lambda j, i: (0, j + nj))],
        out_specs=pl.BlockSpec((tm, tn), lambda j, i: (i, j)),
        out_shape=jax.ShapeDtypeStruct((s, dff), BF16),
        scratch_shapes=[pltpu.VMEM((tm + FFN_HALO, tn), F32), pltpu.VMEM((tm + FFN_HALO, tn), F32)],
        compiler_params=_cparams(("parallel", "arbitrary")),
        name="ffn_in",
    )(u, w_in, w_in, cw, cw, cb, cb)


def _conformer_body(val_ref, gate_ref, valh_ref, gateh_ref, dww_ref, dwb_ref, lnw_ref, lnb_ref,
                    pww_ref, pwb_ref, o_ref, hs, ys, hsb, *, tm, ch):
    i = pl.program_id(0)
    halo = valh_ref[...].astype(F32) * jax.nn.sigmoid(gateh_ref[...].astype(F32))
    hs[0:CONV_HALO, :] = jnp.where(i > 0, halo, 0.0)
    hs[CONV_HALO:CONV_HALO + tm, :] = val_ref[...].astype(F32) * jax.nn.sigmoid(gate_ref[...].astype(F32))

    nrow = tm + CONV_HALO
    for b in range(1, 8):
        hsb[b - 1, 8:nrow, :] = hs[8 - b:nrow - b, :]

    rs, ls = 32, 512
    for r0 in range(0, tm, rs):
        for l0 in range(0, ch, ls):
            acc = jnp.broadcast_to(dwb_ref[0:1, l0:l0 + ls], (rs, ls))
            for shift in range(CONV_WIDTH):
                a, b = divmod(shift, 8)
                j = CONV_WIDTH - 1 - shift
                src = hs if b == 0 else hsb.at[b - 1]
                start = CONV_HALO + r0 - 8 * a
                acc = acc + dww_ref[j:j + 1, l0:l0 + ls] * src[start:start + rs, l0:l0 + ls]
            ys[r0:r0 + rs, l0:l0 + ls] = acc

    y = ys[...]
    mu = jnp.mean(y, axis=-1, keepdims=True)
    yc = y - mu
    var = jnp.mean(yc * yc, axis=-1, keepdims=True)
    z = (yc * lax.rsqrt(var + LN_EPS)) * lnw_ref[...] + lnb_ref[...]
    z = z * jax.nn.sigmoid(z)
    o_ref[...] = (_dot(z.astype(BF16), pww_ref[...]) + pwb_ref[...]).astype(o_ref.dtype)


def _conformer(proj, val_blk, gate_blk, dw_w, dw_b, ln_w, ln_b, pw_w, pw_b, tm):
    s = proj.shape[0]
    ch = dw_w.shape[1]
    hb = tm // CONV_HALO
    row = lambda v: v.reshape(1, ch)
    return pl.pallas_call(
        functools.partial(_conformer_body, tm=tm, ch=ch),
        grid=(s // tm,),
        in_specs=[pl.BlockSpec((tm, ch), lambda i: (i, val_blk)),
                  pl.BlockSpec((tm, ch), lambda i: (i, gate_blk)),
                  pl.BlockSpec((CONV_HALO, ch), lambda i: (jnp.maximum(i * hb - 1, 0), val_blk)),
                  pl.BlockSpec((CONV_HALO, ch), lambda i: (jnp.maximum(i * hb - 1, 0), gate_blk)),
                  pl.BlockSpec((32, ch), lambda i: (0, 0)),
                  pl.BlockSpec((1, ch), lambda i: (0, 0)),
                  pl.BlockSpec((1, ch), lambda i: (0, 0)),
                  pl.BlockSpec((1, ch), lambda i: (0, 0)),
                  pl.BlockSpec((ch, ch), lambda i: (0, 0)),
                  pl.BlockSpec((1, ch), lambda i: (0, 0))],
        out_specs=pl.BlockSpec((tm, ch), lambda i: (i, 0)),
        out_shape=jax.ShapeDtypeStruct((s, ch), BF16),
        scratch_shapes=[pltpu.VMEM((tm + CONV_HALO, ch), F32), pltpu.VMEM((tm, ch), F32),
                        pltpu.VMEM((7, tm + CONV_HALO, ch), F32)],
        compiler_params=_cparams(("parallel",)),
        name="conformer",
    )(proj, proj, proj, proj, jnp.pad(dw_w, ((0, 32 - CONV_WIDTH), (0, 0))), row(dw_b), row(ln_w), row(ln_b),
      pw_w.astype(BF16), row(pw_b))


def _rwkv_body(r_ref, k_ref, v_ref, l_ref, rh_ref, kh_ref, vh_ref, lh_ref,
               mur_ref, muk_ref, muv_ref, mul_ref, wup_ref, aup_ref, gup_ref,
               w0_ref, a0_ref, kk_ref, ka_ref, rk_ref, lnw_ref, lnb_ref,
               e1_ref, e2_ref, tri_ref, o_ref,
               sh, shl, rt_s, at_s, kt_s, bt_s, kw_s, bw_s, v_s, dc_s, y_s, g_s, *, tm, dim):
    i = pl.program_id(0)
    nc = tm // RWKV_CHUNK
    c = RWKV_CHUNK

    @pl.when(i == 0)
    def _():
        g_s[...] = jnp.zeros(g_s.shape, F32)

    def shifted_mix(x_ref, xh_ref, mu_ref, buf):
        x = x_ref[...].astype(F32)
        last = xh_ref[15:16, :].astype(F32)
        buf[7:8, :] = jnp.where(i > 0, last, 0.0)
        buf[8:8 + tm, :] = x
        prev = buf[7:7 + tm, :]
        return x + (prev - x) * mu_ref[...]

    def headsum(x):
        hi, lo = _split2(x)
        s = _dot(hi, e1_ref[...]) + _dot(lo, e1_ref[...])
        shi, slo = _split2(s)
        return _dot(shi, e2_ref[...]) + _dot(slo, e2_ref[...])

    r = shifted_mix(r_ref, rh_ref, mur_ref, sh)
    k = shifted_mix(k_ref, kh_ref, muk_ref, sh)
    v = shifted_mix(v_ref, vh_ref, muv_ref, sh)
    lo_ = shifted_mix(l_ref, lh_ref, mul_ref, shl)

    dw = _dot(jnp.tanh(lo_).astype(BF16), wup_ref[...])
    da = _dot(lo_.astype(BF16), aup_ref[...])
    g = _dot(jax.nn.sigmoid(lo_).astype(BF16), gup_ref[...])
    z = -(w0_ref[...] + dw)
    softplus = jnp.maximum(z, 0.0) + jnp.log(1.0 + jnp.exp(-jnp.abs(z)))
    logw = -jnp.exp(-softplus - 0.5)
    a = jax.nn.sigmoid(a0_ref[...] + da)
    kkr = k * kk_ref[...]
    kk = kkr / jnp.maximum(jnp.sqrt(headsum(kkr * kkr)), 1e-12)
    k2 = k * (1.0 + (a - 1.0) * ka_ref[...])
    bonus = headsum(r * k2 * rk_ref[...]) * v
    a_s = -kk
    b_s = kk * a
    v_s[...] = v

    tri = tri_ref[...]
    for ci in range(nc):
        sl = slice(ci * c, (ci + 1) * c)
        lw = logw[sl]
        h3, m3, l3 = _split3(lw)
        li = _dot(tri, h3) + _dot(tri, m3) + _dot(tri, l3)
        lc = li[c - 1:c, :]
        e_neg = jnp.exp(-li)
        e_end = jnp.exp(lc - li)
        rt_s[sl, :] = r[sl] * jnp.exp(li)
        at_s[sl, :] = a_s[sl] * jnp.exp(li - lw)
        kt_s[sl, :] = k2[sl] * e_neg
        bt_s[sl, :] = b_s[sl] * e_neg
        kw_s[sl, :] = k2[sl] * e_end
        bw_s[sl, :] = b_s[sl] * e_end
        dc_s[ci * 8:(ci + 1) * 8, :] = jnp.broadcast_to(jnp.exp(lc), (8, dim))

    gl = GROUP_LANES
    t_idx = lax.broadcasted_iota(jnp.int32, (c, gl), 0)
    s_idx = jnp.bitwise_and(lax.broadcasted_iota(jnp.int32, (c, gl), 1), HEAD_DIM - 1)
    strict = t_idx > s_idx
    incl = t_idx >= s_idx
    eye = (t_idx == s_idx).astype(F32)
    head_shift = HEAD_DIM.bit_length() - 1
    bd_mask = (jnp.right_shift(lax.broadcasted_iota(jnp.int32, (gl, gl), 0), head_shift)
               == jnp.right_shift(lax.broadcasted_iota(jnp.int32, (gl, gl), 1), head_shift))

    def bd(x):
        return jnp.where(bd_mask, jnp.concatenate([x] * HEADS_PER_GROUP, axis=0), 0.0).astype(BF16)

    def fold(x):
        xm = jnp.where(bd_mask, x, 0.0)
        out = xm[0:HEAD_DIM]
        for q in range(1, HEADS_PER_GROUP):
            out = out + xm[q * HEAD_DIM:(q + 1) * HEAD_DIM]
        return out

    ngrp = dim // gl
    probs = [(ci, q) for ci in range(nc) for q in range(ngrp)]

    def rows_of(ci):
        return slice(ci * c, (ci + 1) * c)

    def lanes_of(q):
        return slice(q * gl, (q + 1) * gl)

    n_ab, a_rb, xl, yv = {}, {}, {}, {}
    for pr in probs:
        rw, ln = rows_of(pr[0]), lanes_of(pr[1])
        lhs = jnp.concatenate([at_s[rw, ln], rt_s[rw, ln]], axis=0).astype(BF16)
        pb = _dot_nt(lhs, bd(bt_s[rw, ln]))
        pk = _dot_nt(lhs, bd(kt_s[rw, ln]))
        n_ab[pr] = jnp.where(strict, pb[0:c], 0.0)
        a_rb[pr] = jnp.where(incl, pb[c:2 * c], 0.0).astype(BF16)
        a_k = jnp.concatenate([jnp.where(strict, pk[0:c], 0.0), jnp.where(incl, pk[c:2 * c], 0.0)], axis=0)
        av = _dot(a_k.astype(BF16), bd(v_s[rw, ln]))
        xl[pr] = av[0:c]
        yv[pr] = av[c:2 * c]

    tinv = {pr: eye + n_ab[pr] for pr in probs}
    mpow = {pr: _dot(n_ab[pr].astype(BF16), bd(n_ab[pr])) for pr in probs}
    for it in range(4):
        for pr in probs:
            res = _dot(jnp.concatenate([mpow[pr], tinv[pr]], axis=0).astype(BF16), bd(mpow[pr]))
            mpow[pr] = res[0:c]
            tinv[pr] = tinv[pr] + res[c:2 * c]
    tinv = {pr: (tinv[pr] + _dot(tinv[pr].astype(BF16), bd(mpow[pr]))).astype(BF16) for pr in probs}

    ta, ul = {}, {}
    for pr in probs:
        rw, ln = rows_of(pr[0]), lanes_of(pr[1])
        ta[pr] = _dot(tinv[pr], bd(at_s[rw, ln]))
        ul[pr] = _dot(tinv[pr], bd(xl[pr]))
    ra, yl, bd_q, cc = {}, {}, {}, {}
    for pr in probs:
        rw, ln = rows_of(pr[0]), lanes_of(pr[1])
        ra[pr] = (rt_s[rw, ln] + _dot(a_rb[pr], bd(ta[pr]))).astype(BF16)
        yl[pr] = _dot(a_rb[pr], bd(ul[pr])) + yv[pr]
        bw = bw_s[rw, ln].astype(BF16)
        bd_q[pr] = bd(fold(_dot(ta[pr].T.astype(BF16), bw)))
        uv = jnp.concatenate([ul[pr], v_s[rw, ln]], axis=0)
        bk = jnp.concatenate([bw, kw_s[rw, ln].astype(BF16)], axis=0)
        cc[pr] = fold(_dot(uv.T.astype(BF16), bk))

    state = [g_s[:, lanes_of(q)] for q in range(ngrp)]
    for ci in range(nc):
        for q in range(ngrp):
            pr = (ci, q)
            gg = state[q]
            y_s[rows_of(ci), lanes_of(q)] = _dot_nt(ra[pr], bd(gg)) + yl[pr]
            gq = _dot(jnp.concatenate(_split2(gg), axis=0), bd_q[pr])
            dc = dc_s[ci * 8:ci * 8 + 1, lanes_of(q)]
            state[q] = dc * gg + (gq[0:HEAD_DIM] + gq[HEAD_DIM:2 * HEAD_DIM]) + cc[pr]
    for q in range(ngrp):
        g_s[:, lanes_of(q)] = state[q]

    y = y_s[...]
    inv_n = 1.0 / HEAD_DIM
    mu = headsum(y) * inv_n
    yc = y - mu
    var = headsum(yc * yc) * inv_n
    yn = (yc * lax.rsqrt(var + RWKV_LNX_EPS)) * lnw_ref[...] + lnb_ref[...]
    o_ref[...] = ((yn + bonus) * g).astype(o_ref.dtype)


def _rwkv(proj, blk_r, blk_k, blk_v, blk_l, mu, w_up, w0, a_up, a0, g_up, k_k, k_a, r_k, lnx_w, lnx_b, tm):
    s = proj.shape[0]
    dim = w0.shape[0]
    nh = dim // HEAD_DIM
    hb = tm // 16
    row = lambda x: x.reshape(1, -1)
    mu_r, mu_k, mu_v = row(mu[0:dim]), row(mu[dim:2 * dim]), row(mu[2 * dim:3 * dim])
    nl = W_LORA + A_LORA + G_LORA
    mu_l = jnp.pad(row(mu[3 * dim:]), ((0, 0), (0, LORA_PAD - nl)))
    wup = jnp.pad(w_up, ((0, LORA_PAD - W_LORA), (0, 0))).astype(BF16)
    aup = jnp.pad(a_up, ((W_LORA, LORA_PAD - W_LORA - A_LORA), (0, 0))).astype(BF16)
    gup = jnp.pad(g_up, ((W_LORA + A_LORA, LORA_PAD - nl), (0, 0))).astype(BF16)
    head_of_lane = jnp.arange(dim) // HEAD_DIM
    e1 = (head_of_lane[:, None] == jnp.arange(LANES)[None, :]).astype(BF16)
    e2 = e1.T
    tri = jnp.tril(jnp.ones((RWKV_CHUNK, RWKV_CHUNK), BF16))
    nc = tm // RWKV_CHUNK

    def tile(blk, w):
        return pl.BlockSpec((tm, w), lambda i: (i, blk))

    def halo(blk, w):
        return pl.BlockSpec((16, w), lambda i: (jnp.maximum(i * hb - 1, 0), blk))

    def const(shape):
        return pl.BlockSpec(shape, lambda i: (0, 0))

    big = lambda: pltpu.VMEM((tm, dim), F32)
    return pl.pallas_call(
        functools.partial(_rwkv_body, tm=tm, dim=dim),
        grid=(s // tm,),
        in_specs=[tile(blk_r, dim), tile(blk_k, dim), tile(blk_v, dim), tile(blk_l, LORA_PAD),
                  halo(blk_r, dim), halo(blk_k, dim), halo(blk_v, dim), halo(blk_l, LORA_PAD),
                  const((1, dim)), const((1, dim)), const((1, dim)), const((1, LORA_PAD)),
                  const((LORA_PAD, dim)), const((LORA_PAD, dim)), const((LORA_PAD, dim)),
                  const((1, dim)), const((1, dim)), const((1, dim)), const((1, dim)), const((1, dim)),
                  const((1, dim)), const((1, dim)),
                  const((dim, LANES)), const((LANES, dim)), const((RWKV_CHUNK, RWKV_CHUNK))],
        out_specs=pl.BlockSpec((tm, dim), lambda i: (i, 0)),
        out_shape=jax.ShapeDtypeStruct((s, dim), BF16),
        scratch_shapes=[pltpu.VMEM((tm + 8, dim), F32), pltpu.VMEM((tm + 8, LORA_PAD), F32),
                        big(), big(), big(), big(), big(), big(), big(),
                        pltpu.VMEM((nc * 8, dim), F32), big(),
                        pltpu.VMEM((HEAD_DIM, dim), F32)],
        compiler_params=_cparams(("arbitrary",)),
        name="rwkv7",
    )(proj, proj, proj, proj, proj, proj, proj, proj,
      mu_r, mu_k, mu_v, mu_l, wup, aup, gup,
      row(w0), row(a0), row(k_k), row(k_a), row(r_k), row(lnx_w), row(lnx_b), e1, e2, tri)


def _t5_bucket_table():
    max_exact = N_BUCKETS // 2
    qi = jnp.arange(BLOCK)[:, None]
    kj = jnp.arange(2 * BLOCK)[None, :]
    dist = qi + BLOCK - kj
    in_window = (dist >= 0) & (dist < WINDOW)
    d = jnp.maximum(dist, 0)
    log_ratio = jnp.log(jnp.maximum(d, 1).astype(F32) / max_exact) / math.log(MAX_DISTANCE / max_exact)
    large = max_exact + (log_ratio * (N_BUCKETS - max_exact)).astype(jnp.int32)
    bucket = jnp.where(d < max_exact, d, jnp.minimum(large, N_BUCKETS - 1))
    return jnp.where(in_window, bucket, -1).astype(jnp.int32)


def _bias_body(bucket_ref, rb_ref, o_ref):
    h = pl.program_id(0)
    bucket = bucket_ref[...]
    acc = jnp.zeros(bucket.shape, F32)
    for b in range(N_BUCKETS):
        acc = jnp.where(bucket == b, rb_ref[b, h], acc)
    o_ref[0] = jnp.where(bucket < 0, -jnp.inf, acc)


def _attn_bias(rel_bias):
    nh = rel_bias.shape[1]
    return pl.pallas_call(
        _bias_body,
        grid=(nh,),
        in_specs=[pl.BlockSpec((BLOCK, 2 * BLOCK), lambda h: (0, 0)),
                  pl.BlockSpec(memory_space=pltpu.SMEM)],
        out_specs=pl.BlockSpec((1, BLOCK, 2 * BLOCK), lambda h: (h, 0, 0)),
        out_shape=jax.ShapeDtypeStruct((nh, BLOCK, 2 * BLOCK), F32),
        compiler_params=_cparams(("arbitrary",)),
        name="attn_bias",
    )(_t5_bucket_table(), rel_bias)


def _attn_body(q_ref, k_ref, v_ref, kh_ref, vh_ref, bias_ref, sink_ref, o_ref, kb, vb, kbs, vbs, *, tq, nq, nkv):
    i = pl.program_id(0)
    nblk = tq // BLOCK
    kvw = nkv * HEAD_DIM
    kb[0:BLOCK, :] = kh_ref[...]
    kb[BLOCK:BLOCK + tq, :] = k_ref[...]
    vb[0:BLOCK, :] = vh_ref[...]
    vb[BLOCK:BLOCK + tq, :] = v_ref[...]
    kbs[...] = pltpu.roll(kb[...].astype(F32), HEAD_DIM, axis=1).astype(BF16)
    vbs[...] = pltpu.roll(vb[...].astype(F32), HEAD_DIM, axis=1).astype(BF16)

    lane = lax.broadcasted_iota(jnp.int32, (BLOCK, LANES), 1)
    low_half = lane < HEAD_DIM
    key_in_own_block = lax.broadcasted_iota(jnp.int32, (BLOCK, 2 * BLOCK), 1) >= BLOCK
    scale = HEAD_DIM ** -0.5

    def block_step(b, carry):
        r0 = pl.multiple_of(b * BLOCK, BLOCK)
        has_prev = (i * nblk + b) > 0
        key_ok = jnp.logical_or(has_prev, key_in_own_block)
        kwin = pl.ds(r0, 2 * BLOCK)
        for pair in range(nq // 2):
            g = (2 * pair) // ATT_GROUP
            q2 = q_ref[pl.ds(r0, BLOCK), pair * LANES:(pair + 1) * LANES]
            halves = []
            for par in range(2):
                h = 2 * pair + par
                if (g % 2) == par:
                    kk, vv, pos = kb, vb, g
                else:
                    kk, vv, pos = kbs, vbs, (g + 1) % nkv
                kv_ln = slice((pos // 2) * LANES, (pos // 2 + 1) * LANES)
                qm = jnp.where(low_half if par == 0 else jnp.logical_not(low_half), q2, jnp.zeros_like(q2))
                logits = _dot_nt(qm, kk[kwin, kv_ln]) * scale + bias_ref[h]
                logits = jnp.where(key_ok, logits, -jnp.inf)
                sink = sink_ref[h]
                mx = jnp.maximum(jnp.max(logits, axis=-1, keepdims=True), sink)
                p = jnp.exp(logits - mx)
                denom = jnp.sum(p, axis=-1, keepdims=True) + jnp.exp(sink - mx)
                halves.append(_dot(p.astype(BF16), vv[kwin, kv_ln]) / denom)
            o_ref[pl.ds(r0, BLOCK), pair * LANES:(pair + 1) * LANES] = (
                jnp.where(low_half, halves[0], halves[1]).astype(o_ref.dtype))
        return carry

    lax.fori_loop(0, nblk, block_step, 0)


def _attention(proj, q_blk, k_blk, v_blk, nq, nkv, bias, sinks, tq):
    s = proj.shape[0]
    qw = nq * HEAD_DIM
    kvw = nkv * HEAD_DIM
    hb = tq // BLOCK
    return pl.pallas_call(
        functools.partial(_attn_body, tq=tq, nq=nq, nkv=nkv),
        grid=(s // tq,),
        in_specs=[pl.BlockSpec((tq, qw), lambda i: (i, q_blk)),
                  pl.BlockSpec((tq, kvw), lambda i: (i, k_blk)),
                  pl.BlockSpec((tq, kvw), lambda i: (i, v_blk)),
                  pl.BlockSpec((BLOCK, kvw), lambda i: (jnp.maximum(i * hb - 1, 0), k_blk)),
                  pl.BlockSpec((BLOCK, kvw), lambda i: (jnp.maximum(i * hb - 1, 0), v_blk)),
                  pl.BlockSpec((nq, BLOCK, 2 * BLOCK), lambda i: (0, 0, 0)),
                  pl.BlockSpec(memory_space=pltpu.SMEM)],
        out_specs=pl.BlockSpec((tq, qw), lambda i: (i, 0)),
        out_shape=jax.ShapeDtypeStruct((s, qw), BF16),
        scratch_shapes=[pltpu.VMEM((tq + BLOCK, kvw), BF16), pltpu.VMEM((tq + BLOCK, kvw), BF16),
                        pltpu.VMEM((tq + BLOCK, kvw), BF16), pltpu.VMEM((tq + BLOCK, kvw), BF16)],
        compiler_params=_cparams(("parallel",)),
        name="swa",
    )(proj, proj, proj, proj, proj, bias, sinks)


def _tile(n, pref):
    t = min(n, pref)
    assert n % t == 0
    return t


def kernel(x, mix_norm_w, w_in, conv_dw_w, conv_dw_b, conv_ln_w, conv_ln_b, conv_pw_w, conv_pw_b, rwkv_mu, rwkv_w_up, rwkv_w0, rwkv_a_up, rwkv_a0, rwkv_g_up, rwkv_k_k, rwkv_k_a, rwkv_r_k, rwkv_lnx_w, rwkv_lnx_b, attn_sinks, rel_bias, w_out, ffn_norm_w, ffn_w_in, ffn_conv_w, ffn_conv_b, ffn_w_out, final_norm_w):
    bsz, s, d = x.shape
    assert bsz == 1
    depth = w_in.shape[0]
    cc = conv_dw_w.shape[2]
    rd = rwkv_w0.shape[1]
    nq = attn_sinks.shape[1]
    nkv = nq // ATT_GROUP
    ad, akv = nq * HEAD_DIM, nkv * HEAD_DIM
    nl = W_LORA + A_LORA + G_LORA
    dff = ffn_w_out.shape[1]
    assert cc == rd and ad == 2 * cc and 4 * akv == cc and w_in.shape[2] == 2 * cc + 3 * rd + nl + ad + 2 * akv
    assert s % 512 == 0

    o_conv, o_rwkv, o_att = 0, 2 * cc, 2 * cc + 3 * rd + nl
    n_proj = ad + 2 * cc + 3 * rd + 2 * akv + LORA_PAD
    n_proj_pad = -(-n_proj // 1024) * 1024
    col_q = 0
    col_val, col_gate = ad // cc, ad // cc + 1
    col_r, col_k, col_v = ad // cc + 2, ad // cc + 3, ad // cc + 4
    off_kv = ad + 2 * cc + 3 * rd
    col_ak, col_av = off_kv // akv, off_kv // akv + 1
    off_l = off_kv + 2 * akv
    assert off_l % LORA_PAD == 0
    col_l = off_l // LORA_PAD

    dff_pad = -(-dff // 1024) * 1024

    def pad_ff(a):
        halves = a.reshape(a.shape[:-1] + (2, dff))
        halves = jnp.pad(halves, [(0, 0)] * (a.ndim - 1) + [(0, 0), (0, dff_pad - dff)])
        return halves.reshape(a.shape[:-1] + (2 * dff_pad,))

    w_in_b = jnp.concatenate(
        [w_in[:, :, o_att:o_att + ad], w_in[:, :, o_conv:o_conv + 2 * cc], w_in[:, :, o_rwkv:o_rwkv + 3 * rd],
         w_in[:, :, o_att + ad:o_att + ad + 2 * akv], w_in[:, :, o_rwkv + 3 * rd:o_rwkv + 3 * rd + nl],
         jnp.zeros((depth, d, n_proj_pad - n_proj + LORA_PAD - nl), w_in.dtype)], axis=2).astype(BF16)
    w_out_b = w_out.astype(BF16)
    ffn_w_in_b = pad_ff(ffn_w_in.astype(BF16))
    ffn_w_out_b = jnp.pad(ffn_w_out.astype(BF16), ((0, 0), (0, dff_pad - dff), (0, 0)))
    ffn_conv_w_p = pad_ff(ffn_conv_w)
    ffn_conv_b_p = pad_ff(ffn_conv_b)

    tm = _tile(s, 1024)
    bias = _attn_bias(rel_bias)
    h = x.reshape(s, d)
    for l in range(depth):
        u = _rmsnorm(h, mix_norm_w[l], BF16, _tile(s, 512))
        proj = _matmul(u, w_in_b, l, BF16, tm, 1024)
        y_conv = _conformer(proj, col_val, col_gate, conv_dw_w[l], conv_dw_b[l], conv_ln_w[l], conv_ln_b[l],
                            conv_pw_w[l], conv_pw_b[l], _tile(s, 256))
        y_rwkv = _rwkv(proj, col_r, col_k, col_v, col_l, rwkv_mu[l], rwkv_w_up[l], rwkv_w0[l], rwkv_a_up[l],
                       rwkv_a0[l], rwkv_g_up[l], rwkv_k_k[l], rwkv_k_a[l], rwkv_r_k[l].reshape(-1),
                       rwkv_lnx_w[l], rwkv_lnx_b[l], _tile(s, 256))
        y_att = _attention(proj, col_q, col_ak, col_av, nq, nkv, bias, attn_sinks[l], _tile(s, 512))
        h = _mix_out(y_conv, y_rwkv, y_att, w_out_b, l, h, tm, 512)
        u = _rmsnorm(h, ffn_norm_w[l], BF16, _tile(s, 512))
        gact = _ffn_in(u, ffn_w_in_b, l, ffn_conv_w_p[l], ffn_conv_b_p[l], dff_pad, tm, 512)
        h = _ffn_out(gact, ffn_w_out_b, l, h, tm, 1024, dff_pad // 4)
    out = _rmsnorm(h, final_norm_w, F32, _tile(s, 512))
    return out.reshape(bsz, s, d)
```

```python
import functools
import math

import jax
import jax.numpy as jnp
from jax import lax
from jax.experimental import pallas as pl
from jax.experimental.pallas import tpu as pltpu

F32 = jnp.float32
BF16 = jnp.bfloat16

HEAD_DIM = 64
CONV_WIDTH = 31
W_LORA = 64
A_LORA = 64
G_LORA = 160
RWKV_LNX_EPS = 64e-5
ATT_GROUP = 8
WINDOW = 128
BLOCK = 128
N_BUCKETS = 32
MAX_DISTANCE = 128
FFN_CONV_WIDTH = 3
RMS_EPS = 1e-6
LN_EPS = 1e-5

LANES = 128
LORA_PAD = 384
RWKV_CHUNK = 64
HEADS_PER_GROUP = 4
GROUP_LANES = HEADS_PER_GROUP * HEAD_DIM
CONV_HALO = 32
FFN_HALO = 8
FFN_SUB_ROWS = 256
CAST_ROWS = 256
VMEM_LIMIT = 56 * 1024 * 1024


def _cparams(sem):
    return pltpu.CompilerParams(dimension_semantics=sem, vmem_limit_bytes=VMEM_LIMIT)


def _dot(a, b):
    return jnp.dot(a, b, preferred_element_type=F32)


def _dot_nt(a, b):
    return lax.dot_general(a, b, (((1,), (1,)), ((), ())), preferred_element_type=F32)


def _split2(x):
    hi = x.astype(BF16)
    lo = (x - hi.astype(F32)).astype(BF16)
    return hi, lo


def _split3(x):
    hi = x.astype(BF16)
    r1 = x - hi.astype(F32)
    mid = r1.astype(BF16)
    lo = (r1 - mid.astype(F32)).astype(BF16)
    return hi, mid, lo


def _rmsnorm_body(x_ref, w_ref, o_ref):
    x = x_ref[...]
    y = x * lax.rsqrt(jnp.mean(x * x, axis=-1, keepdims=True) + RMS_EPS)
    o_ref[...] = (y * w_ref[...]).astype(o_ref.dtype)


def _rmsnorm(x, w, out_dtype, tm):
    s, d = x.shape
    return pl.pallas_call(
        _rmsnorm_body,
        grid=(s // tm,),
        in_specs=[pl.BlockSpec((tm, d), lambda i: (i, 0)),
                  pl.BlockSpec((1, d), lambda i: (0, 0))],
        out_specs=pl.BlockSpec((tm, d), lambda i: (i, 0)),
        out_shape=jax.ShapeDtypeStruct((s, d), out_dtype),
        compiler_params=_cparams(("parallel",)),
        name="rmsnorm",
    )(x, w.reshape(1, d))


def _row_scale(ssq_ref, rows, d):
    return lax.rsqrt(ssq_ref[rows, 0:1] * (1.0 / d) + RMS_EPS)


def _emit_h(hn, col_block, o_ref, ob_ref, ssq_ref):
    o_ref[...] = hn
    ob_ref[...] = hn.astype(ob_ref.dtype)
    part = jnp.broadcast_to(jnp.sum(hn * hn, axis=-1, keepdims=True), ssq_ref.shape)

    @pl.when(col_block == 0)
    def _():
        ssq_ref[...] = part

    @pl.when(col_block > 0)
    def _():
        ssq_ref[...] += part


def _h_prep_body(x_ref, ob_ref, ssq_ref):
    x = x_ref[...]
    ob_ref[...] = x.astype(ob_ref.dtype)
    ssq_ref[...] = jnp.broadcast_to(jnp.sum(x * x, axis=-1, keepdims=True), ssq_ref.shape)


def _h_prep(x, tm):
    s, d = x.shape
    return pl.pallas_call(
        _h_prep_body,
        grid=(s // tm,),
        in_specs=[pl.BlockSpec((tm, d), lambda i: (i, 0))],
        out_specs=[pl.BlockSpec((tm, d), lambda i: (i, 0)), pl.BlockSpec((tm, LANES), lambda i: (i, 0))],
        out_shape=[jax.ShapeDtypeStruct((s, d), BF16), jax.ShapeDtypeStruct((s, LANES), F32)],
        compiler_params=_cparams(("parallel",)),
        name="h_prep",
    )(x)


def _cast_pad_cols_body(x_ref, g_ref, o_ref, *, n):
    o_ref[:, 0:n] = (x_ref[...] * g_ref[...]).astype(o_ref.dtype)
    o_ref[:, n:] = jnp.zeros((o_ref.shape[0], o_ref.shape[1] - n), o_ref.dtype)


def _cast_pad_halves(w, row_gain, n, n_pad):
    nl, r, _ = w.shape
    assert n % LANES == 0 and n_pad % LANES == 0 and r % CAST_ROWS == 0
    return pl.pallas_call(
        functools.partial(_cast_pad_cols_body, n=n),
        grid=(nl, r // CAST_ROWS, 2),
        in_specs=[pl.BlockSpec((None, CAST_ROWS, n), lambda l, i, hf: (l, i, hf)),
                  pl.BlockSpec((None, CAST_ROWS, 1), lambda l, i, hf: (l, i, 0))],
        out_specs=pl.BlockSpec((None, CAST_ROWS, n_pad), lambda l, i, hf: (l, i, hf)),
        out_shape=jax.ShapeDtypeStruct((nl, r, 2 * n_pad), BF16),
        compiler_params=_cparams(("parallel", "parallel", "parallel")),
        name="cast_pad_cols",
    )(w, row_gain.reshape(nl, r, 1))


def _cast_pad_rows_body(x_ref, o_ref, *, n_valid):
    i = pl.program_id(1)

    @pl.when(i < n_valid)
    def _():
        o_ref[...] = x_ref[...].astype(o_ref.dtype)

    @pl.when(i >= n_valid)
    def _():
        o_ref[...] = jnp.zeros(o_ref.shape, o_ref.dtype)


def _cast_pad_rows(w, r_pad):
    nl, r, c = w.shape
    assert r % CAST_ROWS == 0 and r_pad % CAST_ROWS == 0
    n_valid = r // CAST_ROWS
    return pl.pallas_call(
        functools.partial(_cast_pad_rows_body, n_valid=n_valid),
        grid=(nl, r_pad // CAST_ROWS),
        in_specs=[pl.BlockSpec((None, CAST_ROWS, c), lambda l, i: (l, jnp.minimum(i, n_valid - 1), 0))],
        out_specs=pl.BlockSpec((None, CAST_ROWS, c), lambda l, i: (l, i, 0)),
        out_shape=jax.ShapeDtypeStruct((nl, r_pad, c), BF16),
        compiler_params=_cparams(("parallel", "parallel")),
        name="cast_pad_rows",
    )(w)


def _mm_body(x_ref, w_ref, ssq_ref, o_ref):
    scale = _row_scale(ssq_ref, slice(None), x_ref.shape[1])
    o_ref[...] = (_dot(x_ref[...], w_ref[...]) * scale).astype(o_ref.dtype)


def _matmul(x, ssq, w, layer, out_dtype, tm, tn):
    m, k = x.shape
    n = w.shape[2]
    return pl.pallas_call(
        _mm_body,
        grid=(m // tm, n // tn),
        in_specs=[pl.BlockSpec((tm, k), lambda i, j: (i, 0)),
                  pl.BlockSpec((None, k, tn), lambda i, j: (layer, 0, j)),
                  pl.BlockSpec((tm, ssq.shape[1]), lambda i, j: (i, 0))],
        out_specs=pl.BlockSpec((tm, tn), lambda i, j: (i, j)),
        out_shape=jax.ShapeDtypeStruct((m, n), out_dtype),
        compiler_params=_cparams(("parallel", "arbitrary")),
        name="in_proj",
    )(x, w, ssq)


def _h_outputs(m, n, tm, tn, index, row_index):
    specs = [pl.BlockSpec((tm, tn), index), pl.BlockSpec((tm, tn), index), pl.BlockSpec((tm, LANES), row_index)]
    shapes = [jax.ShapeDtypeStruct((m, n), F32), jax.ShapeDtypeStruct((m, n), BF16),
              jax.ShapeDtypeStruct((m, LANES), F32)]
    return specs, shapes


def _mix_out_body(yc_ref, yr_ref, ya_ref, wc_ref, wr_ref, wa_ref, h_ref, o_ref, ob_ref, ssq_ref):
    acc = _dot(yc_ref[...], wc_ref[...])
    acc += _dot(yr_ref[...], wr_ref[...])
    acc += _dot(ya_ref[...], wa_ref[...])
    _emit_h(h_ref[...] + acc, pl.program_id(1), o_ref, ob_ref, ssq_ref)


def _mix_out(y_conv, y_rwkv, y_att, w_out, layer, h, tm, tn):
    m, kc = y_conv.shape
    kr = y_rwkv.shape[1]
    ka = y_att.shape[1]
    assert kc == kr and ka == 2 * kc
    n = w_out.shape[2]
    out_specs, out_shape = _h_outputs(m, n, tm, tn, lambda i, j: (i, j), lambda i, j: (i, 0))
    return pl.pallas_call(
        _mix_out_body,
        grid=(m // tm, n // tn),
        in_specs=[pl.BlockSpec((tm, kc), lambda i, j: (i, 0)),
                  pl.BlockSpec((tm, kr), lambda i, j: (i, 0)),
                  pl.BlockSpec((tm, ka), lambda i, j: (i, 0)),
                  pl.BlockSpec((None, kc, tn), lambda i, j: (layer, 0, j)),
                  pl.BlockSpec((None, kr, tn), lambda i, j: (layer, 1, j)),
                  pl.BlockSpec((None, ka, tn), lambda i, j: (layer, 1, j)),
                  pl.BlockSpec((tm, tn), lambda i, j: (i, j))],
        out_specs=out_specs,
        out_shape=out_shape,
        compiler_params=_cparams(("parallel", "arbitrary")),
        name="mix_out",
    )(y_conv, y_rwkv, y_att, w_out, w_out, w_out, h)


def _ffn_out_body(x_ref, w_ref, h_ref, o_ref, ob_ref, ssq_ref, acc_ref, *, nk):
    k = pl.program_id(2)

    @pl.when(k == 0)
    def _():
        acc_ref[...] = h_ref[...]

    acc_ref[...] += _dot(x_ref[...], w_ref[...])

    @pl.when(k == nk - 1)
    def _():
        _emit_h(acc_ref[...], pl.program_id(1), o_ref, ob_ref, ssq_ref)


def _ffn_out(g, w, layer, h, tm, tn, tk):
    m, kdim = g.shape
    n = w.shape[2]
    nk = kdim // tk
    out_specs, out_shape = _h_outputs(m, n, tm, tn, lambda i, j, k: (i, j), lambda i, j, k: (i, 0))
    return pl.pallas_call(
        functools.partial(_ffn_out_body, nk=nk),
        grid=(m // tm, n // tn, nk),
        in_specs=[pl.BlockSpec((tm, tk), lambda i, j, k: (i, k)),
                  pl.BlockSpec((None, tk, tn), lambda i, j, k: (layer, k, j)),
                  pl.BlockSpec((tm, tn), lambda i, j, k: (i, j))],
        out_specs=out_specs,
        out_shape=out_shape,
        scratch_shapes=[pltpu.VMEM((tm, tn), F32)],
        compiler_params=_cparams(("parallel", "arbitrary", "arbitrary")),
        name="ffn_out",
    )(g, w, h)


def _ffn_in_body(u_ref, ssq_ref, wg_ref, wu_ref, cwg_ref, cwu_ref, cbg_ref, cbu_ref, o_ref, hg_s, hu_s, *, tm):
    m = pl.program_id(1)

    @pl.when(m == 0)
    def _():
        hg_s[0:FFN_HALO, :] = jnp.zeros((FFN_HALO, hg_s.shape[1]), F32)
        hu_s[0:FFN_HALO, :] = jnp.zeros((FFN_HALO, hu_s.shape[1]), F32)

    def conv(s, cw_ref, cb_ref, r0, n):
        y = cw_ref[0:1, :] * s[r0 + FFN_HALO - 2:r0 + FFN_HALO - 2 + n, :]
        y += cw_ref[1:2, :] * s[r0 + FFN_HALO - 1:r0 + FFN_HALO - 1 + n, :]
        y += cw_ref[2:3, :] * s[r0 + FFN_HALO:r0 + FFN_HALO + n, :]
        return y + cb_ref[...]

    sub = min(tm, FFN_SUB_ROWS)
    for r0 in range(0, tm, sub):
        u = u_ref[r0:r0 + sub, :]
        scale = _row_scale(ssq_ref, slice(r0, r0 + sub), u_ref.shape[1])
        hg_s[FFN_HALO + r0:FFN_HALO + r0 + sub, :] = _dot(u, wg_ref[...]) * scale
        hu_s[FFN_HALO + r0:FFN_HALO + r0 + sub, :] = _dot(u, wu_ref[...]) * scale
        gate = conv(hg_s, cwg_ref, cbg_ref, r0, sub)
        up = conv(hu_s, cwu_ref, cbu_ref, r0, sub)
        o_ref[r0:r0 + sub, :] = (gate * jax.nn.sigmoid(gate) * up).astype(o_ref.dtype)
    hg_s[0:FFN_HALO, :] = hg_s[tm:tm + FFN_HALO, :]
    hu_s[0:FFN_HALO, :] = hu_s[tm:tm + FFN_HALO, :]


def _ffn_in(u, ssq, w_in, layer, conv_w, conv_b, dff, tm, tn):
    s, d = u.shape
    nj = dff // tn
    cw = jnp.pad(conv_w, ((0, 8 - FFN_CONV_WIDTH), (0, 0)))
    cb = conv_b.reshape(1, 2 * dff)
    return pl.pallas_call(
        functools.partial(_ffn_in_body, tm=tm),
        grid=(nj, s // tm),
        in_specs=[pl.BlockSpec((tm, d), lambda j, i: (i, 0)),
                  pl.BlockSpec((tm, ssq.shape[1]), lambda j, i: (i, 0)),
                  pl.BlockSpec((None, d, tn), lambda j, i: (layer, 0, j)),
                  pl.BlockSpec((None, d, tn), lambda j, i: (layer, 0, j + nj)),
                  pl.BlockSpec((8, tn), lambda j, i: (0, j)),
                  pl.BlockSpec((8, tn), lambda j, i: (0, j + nj)),
                  pl.BlockSpec((1, tn), lambda j, i: (0, j)),
                  pl.BlockSpec((1, tn), lambda j, i: (0, j + nj))],
        out_specs=pl.BlockSpec((tm, tn), lambda j, i: (i, j)),
        out_shape=jax.ShapeDtypeStruct((s, dff), BF16),
        scratch_shapes=[pltpu.VMEM((tm + FFN_HALO, tn), F32), pltpu.VMEM((tm + FFN_HALO, tn), F32)],
        compiler_params=_cparams(("parallel", "arbitrary")),
        name="ffn_in",
    )(u, ssq, w_in, w_in, cw, cw, cb, cb)


def _conformer_body(val_ref, gate_ref, valh_ref, gateh_ref, dww_ref, dwb_ref, lnw_ref, lnb_ref,
                    pww_ref, pwb_ref, o_ref, hs, ys, hsb, *, tm, ch):
    i = pl.program_id(0)
    halo = valh_ref[...].astype(F32) * jax.nn.sigmoid(gateh_ref[...].astype(F32))
    hs[0:CONV_HALO, :] = jnp.where(i > 0, halo, 0.0)
    hs[CONV_HALO:CONV_HALO + tm, :] = val_ref[...].astype(F32) * jax.nn.sigmoid(gate_ref[...].astype(F32))

    nrow = tm + CONV_HALO
    for b in range(1, 8):
        hsb[b - 1, 8:nrow, :] = hs[8 - b:nrow - b, :]

    rs, ls = 32, 512
    for r0 in range(0, tm, rs):
        for l0 in range(0, ch, ls):
            acc = jnp.broadcast_to(dwb_ref[0:1, l0:l0 + ls], (rs, ls))
            for shift in range(CONV_WIDTH):
                a, b = divmod(shift, 8)
                j = CONV_WIDTH - 1 - shift
                src = hs if b == 0 else hsb.at[b - 1]
                start = CONV_HALO + r0 - 8 * a
                acc = acc + dww_ref[j:j + 1, l0:l0 + ls] * src[start:start + rs, l0:l0 + ls]
            ys[r0:r0 + rs, l0:l0 + ls] = acc

    y = ys[...]
    mu = jnp.mean(y, axis=-1, keepdims=True)
    yc = y - mu
    var = jnp.mean(yc * yc, axis=-1, keepdims=True)
    z = (yc * lax.rsqrt(var + LN_EPS)) * lnw_ref[...] + lnb_ref[...]
    z = z * jax.nn.sigmoid(z)
    o_ref[...] = (_dot(z.astype(BF16), pww_ref[...]) + pwb_ref[...]).astype(o_ref.dtype)


def _conformer(proj, val_blk, gate_blk, dw_w, dw_b, ln_w, ln_b, pw_w, pw_b, tm):
    s = proj.shape[0]
    ch = dw_w.shape[1]
    hb = tm // CONV_HALO
    row = lambda v: v.reshape(1, ch)
    return pl.pallas_call(
        functools.partial(_conformer_body, tm=tm, ch=ch),
        grid=(s // tm,),
        in_specs=[pl.BlockSpec((tm, ch), lambda i: (i, val_blk)),
                  pl.BlockSpec((tm, ch), lambda i: (i, gate_blk)),
                  pl.BlockSpec((CONV_HALO, ch), lambda i: (jnp.maximum(i * hb - 1, 0), val_blk)),
                  pl.BlockSpec((CONV_HALO, ch), lambda i: (jnp.maximum(i * hb - 1, 0), gate_blk)),
                  pl.BlockSpec((32, ch), lambda i: (0, 0)),
                  pl.BlockSpec((1, ch), lambda i: (0, 0)),
                  pl.BlockSpec((1, ch), lambda i: (0, 0)),
                  pl.BlockSpec((1, ch), lambda i: (0, 0)),
                  pl.BlockSpec((ch, ch), lambda i: (0, 0)),
                  pl.BlockSpec((1, ch), lambda i: (0, 0))],
        out_specs=pl.BlockSpec((tm, ch), lambda i: (i, 0)),
        out_shape=jax.ShapeDtypeStruct((s, ch), BF16),
        scratch_shapes=[pltpu.VMEM((tm + CONV_HALO, ch), F32), pltpu.VMEM((tm, ch), F32),
                        pltpu.VMEM((7, tm + CONV_HALO, ch), F32)],
        compiler_params=_cparams(("parallel",)),
        name="conformer",
    )(proj, proj, proj, proj, jnp.pad(dw_w, ((0, 32 - CONV_WIDTH), (0, 0))), row(dw_b), row(ln_w), row(ln_b),
      pw_w.astype(BF16), row(pw_b))


def _rwkv_body(r_ref, k_ref, v_ref, l_ref, rh_ref, kh_ref, vh_ref, lh_ref,
               mur_ref, muk_ref, muv_ref, mul_ref, wup_ref, aup_ref, gup_ref,
               w0_ref, a0_ref, kk_ref, ka_ref, rk_ref, lnw_ref, lnb_ref,
               e1_ref, e2_ref, tri_ref, o_ref,
               sh, shl, rt_s, at_s, kt_s, bt_s, kw_s, bw_s, v_s, dc_s, y_s, g_s, *, tm, dim):
    i = pl.program_id(0)
    nc = tm // RWKV_CHUNK
    c = RWKV_CHUNK

    @pl.when(i == 0)
    def _():
        g_s[...] = jnp.zeros(g_s.shape, F32)

    def shifted_mix(x_ref, xh_ref, mu_ref, buf):
        x = x_ref[...].astype(F32)
        last = xh_ref[15:16, :].astype(F32)
        buf[7:8, :] = jnp.where(i > 0, last, 0.0)
        buf[8:8 + tm, :] = x
        prev = buf[7:7 + tm, :]
        return x + (prev - x) * mu_ref[...]

    def headsum(x):
        hi, lo = _split2(x)
        s = _dot(hi, e1_ref[...]) + _dot(lo, e1_ref[...])
        shi, slo = _split2(s)
        return _dot(shi, e2_ref[...]) + _dot(slo, e2_ref[...])

    r = shifted_mix(r_ref, rh_ref, mur_ref, sh)
    k = shifted_mix(k_ref, kh_ref, muk_ref, sh)
    v = shifted_mix(v_ref, vh_ref, muv_ref, sh)
    lo_ = shifted_mix(l_ref, lh_ref, mul_ref, shl)

    dw = _dot(jnp.tanh(lo_).astype(BF16), wup_ref[...])
    da = _dot(lo_.astype(BF16), aup_ref[...])
    g = _dot(jax.nn.sigmoid(lo_).astype(BF16), gup_ref[...])
    z = -(w0_ref[...] + dw)
    softplus = jnp.maximum(z, 0.0) + jnp.log(1.0 + jnp.exp(-jnp.abs(z)))
    logw = -jnp.exp(-softplus - 0.5)
    a = jax.nn.sigmoid(a0_ref[...] + da)
    kkr = k * kk_ref[...]
    kk = kkr / jnp.maximum(jnp.sqrt(headsum(kkr * kkr)), 1e-12)
    k2 = k * (1.0 + (a - 1.0) * ka_ref[...])
    bonus = headsum(r * k2 * rk_ref[...]) * v
    a_s = -kk
    b_s = kk * a
    v_s[...] = v

    tri = tri_ref[...]
    for ci in range(nc):
        sl = slice(ci * c, (ci + 1) * c)
        lw = logw[sl]
        h3, m3, l3 = _split3(lw)
        li = _dot(tri, h3) + _dot(tri, m3) + _dot(tri, l3)
        lc = li[c - 1:c, :]
        e_neg = jnp.exp(-li)
        e_end = jnp.exp(lc - li)
        rt_s[sl, :] = r[sl] * jnp.exp(li)
        at_s[sl, :] = a_s[sl] * jnp.exp(li - lw)
        kt_s[sl, :] = k2[sl] * e_neg
        bt_s[sl, :] = b_s[sl] * e_neg
        kw_s[sl, :] = k2[sl] * e_end
        bw_s[sl, :] = b_s[sl] * e_end
        dc_s[ci * 8:(ci + 1) * 8, :] = jnp.broadcast_to(jnp.exp(lc), (8, dim))

    gl = GROUP_LANES
    t_idx = lax.broadcasted_iota(jnp.int32, (c, gl), 0)
    s_idx = jnp.bitwise_and(lax.broadcasted_iota(jnp.int32, (c, gl), 1), HEAD_DIM - 1)
    strict = t_idx > s_idx
    incl = t_idx >= s_idx
    eye = (t_idx == s_idx).astype(F32)
    head_shift = HEAD_DIM.bit_length() - 1
    bd_mask = (jnp.right_shift(lax.broadcasted_iota(jnp.int32, (gl, gl), 0), head_shift)
               == jnp.right_shift(lax.broadcasted_iota(jnp.int32, (gl, gl), 1), head_shift))

    def bd(x):
        return jnp.where(bd_mask, jnp.concatenate([x] * HEADS_PER_GROUP, axis=0), 0.0).astype(BF16)

    def fold(x):
        xm = jnp.where(bd_mask, x, 0.0)
        out = xm[0:HEAD_DIM]
        for q in range(1, HEADS_PER_GROUP):
            out = out + xm[q * HEAD_DIM:(q + 1) * HEAD_DIM]
        return out

    ngrp = dim // gl
    probs = [(ci, q) for ci in range(nc) for q in range(ngrp)]

    def rows_of(ci):
        return slice(ci * c, (ci + 1) * c)

    def lanes_of(q):
        return slice(q * gl, (q + 1) * gl)

    n_ab, a_rb, xl, yv = {}, {}, {}, {}
    for pr in probs:
        rw, ln = rows_of(pr[0]), lanes_of(pr[1])
        lhs = jnp.concatenate([at_s[rw, ln], rt_s[rw, ln]], axis=0).astype(BF16)
        pb = _dot_nt(lhs, bd(bt_s[rw, ln]))
        pk = _dot_nt(lhs, bd(kt_s[rw, ln]))
        n_ab[pr] = jnp.where(strict, pb[0:c], 0.0)
        a_rb[pr] = jnp.where(incl, pb[c:2 * c], 0.0).astype(BF16)
        a_k = jnp.concatenate([jnp.where(strict, pk[0:c], 0.0), jnp.where(incl, pk[c:2 * c], 0.0)], axis=0)
        av = _dot(a_k.astype(BF16), bd(v_s[rw, ln]))
        xl[pr] = av[0:c]
        yv[pr] = av[c:2 * c]

    tinv = {pr: eye + n_ab[pr] for pr in probs}
    mpow = {pr: _dot(n_ab[pr].astype(BF16), bd(n_ab[pr])) for pr in probs}
    for it in range(4):
        for pr in probs:
            res = _dot(jnp.concatenate([mpow[pr], tinv[pr]], axis=0).astype(BF16), bd(mpow[pr]))
            mpow[pr] = res[0:c]
            tinv[pr] = tinv[pr] + res[c:2 * c]
    tinv = {pr: (tinv[pr] + _dot(tinv[pr].astype(BF16), bd(mpow[pr]))).astype(BF16) for pr in probs}

    ta, ul = {}, {}
    for pr in probs:
        rw, ln = rows_of(pr[0]), lanes_of(pr[1])
        ta[pr] = _dot(tinv[pr], bd(at_s[rw, ln]))
        ul[pr] = _dot(tinv[pr], bd(xl[pr]))
    ra, yl, bd_q, cc = {}, {}, {}, {}
    for pr in probs:
        rw, ln = rows_of(pr[0]), lanes_of(pr[1])
        ra[pr] = (rt_s[rw, ln] + _dot(a_rb[pr], bd(ta[pr]))).astype(BF16)
        yl[pr] = _dot(a_rb[pr], bd(ul[pr])) + yv[pr]
        bw = bw_s[rw, ln].astype(BF16)
        bd_q[pr] = bd(fold(_dot(ta[pr].T.astype(BF16), bw)))
        uv = jnp.concatenate([ul[pr], v_s[rw, ln]], axis=0)
        bk = jnp.concatenate([bw, kw_s[rw, ln].astype(BF16)], axis=0)
        cc[pr] = fold(_dot(uv.T.astype(BF16), bk))

    state = [g_s[:, lanes_of(q)] for q in range(ngrp)]
    for ci in range(nc):
        for q in range(ngrp):
            pr = (ci, q)
            gg = state[q]
            y_s[rows_of(ci), lanes_of(q)] = _dot_nt(ra[pr], bd(gg)) + yl[pr]
            gq = _dot(jnp.concatenate(_split2(gg), axis=0), bd_q[pr])
            dc = dc_s[ci * 8:ci * 8 + 1, lanes_of(q)]
            state[q] = dc * gg + (gq[0:HEAD_DIM] + gq[HEAD_DIM:2 * HEAD_DIM]) + cc[pr]
    for q in range(ngrp):
        g_s[:, lanes_of(q)] = state[q]

    y = y_s[...]
    inv_n = 1.0 / HEAD_DIM
    mu = headsum(y) * inv_n
    yc = y - mu
    var = headsum(yc * yc) * inv_n
    yn = (yc * lax.rsqrt(var + RWKV_LNX_EPS)) * lnw_ref[...] + lnb_ref[...]
    o_ref[...] = ((yn + bonus) * g).astype(o_ref.dtype)


def _rwkv(proj, blk_r, blk_k, blk_v, blk_l, mu, w_up, w0, a_up, a0, g_up, k_k, k_a, r_k, lnx_w, lnx_b, tm):
    s = proj.shape[0]
    dim = w0.shape[0]
    nh = dim // HEAD_DIM
    hb = tm // 16
    row = lambda x: x.reshape(1, -1)
    mu_r, mu_k, mu_v = row(mu[0:dim]), row(mu[dim:2 * dim]), row(mu[2 * dim:3 * dim])
    nl = W_LORA + A_LORA + G_LORA
    mu_l = jnp.pad(row(mu[3 * dim:]), ((0, 0), (0, LORA_PAD - nl)))
    wup = jnp.pad(w_up, ((0, LORA_PAD - W_LORA), (0, 0))).astype(BF16)
    aup = jnp.pad(a_up, ((W_LORA, LORA_PAD - W_LORA - A_LORA), (0, 0))).astype(BF16)
    gup = jnp.pad(g_up, ((W_LORA + A_LORA, LORA_PAD - nl), (0, 0))).astype(BF16)
    head_of_lane = jnp.arange(dim) // HEAD_DIM
    e1 = (head_of_lane[:, None] == jnp.arange(LANES)[None, :]).astype(BF16)
    e2 = e1.T
    tri = jnp.tril(jnp.ones((RWKV_CHUNK, RWKV_CHUNK), BF16))
    nc = tm // RWKV_CHUNK

    def tile(blk, w):
        return pl.BlockSpec((tm, w), lambda i: (i, blk))

    def halo(blk, w):
        return pl.BlockSpec((16, w), lambda i: (jnp.maximum(i * hb - 1, 0), blk))

    def const(shape):
        return pl.BlockSpec(shape, lambda i: (0, 0))

    big = lambda: pltpu.VMEM((tm, dim), F32)
    return pl.pallas_call(
        functools.partial(_rwkv_body, tm=tm, dim=dim),
        grid=(s // tm,),
        in_specs=[tile(blk_r, dim), tile(blk_k, dim), tile(blk_v, dim), tile(blk_l, LORA_PAD),
                  halo(blk_r, dim), halo(blk_k, dim), halo(blk_v, dim), halo(blk_l, LORA_PAD),
                  const((1, dim)), const((1, dim)), const((1, dim)), const((1, LORA_PAD)),
                  const((LORA_PAD, dim)), const((LORA_PAD, dim)), const((LORA_PAD, dim)),
                  const((1, dim)), const((1, dim)), const((1, dim)), const((1, dim)), const((1, dim)),
                  const((1, dim)), const((1, dim)),
                  const((dim, LANES)), const((LANES, dim)), const((RWKV_CHUNK, RWKV_CHUNK))],
        out_specs=pl.BlockSpec((tm, dim), lambda i: (i, 0)),
        out_shape=jax.ShapeDtypeStruct((s, dim), BF16),
        scratch_shapes=[pltpu.VMEM((tm + 8, dim), F32), pltpu.VMEM((tm + 8, LORA_PAD), F32),
                        big(), big(), big(), big(), big(), big(), big(),
                        pltpu.VMEM((nc * 8, dim), F32), big(),
                        pltpu.VMEM((HEAD_DIM, dim), F32)],
        compiler_params=_cparams(("arbitrary",)),
        name="rwkv7",
    )(proj, proj, proj, proj, proj, proj, proj, proj,
      mu_r, mu_k, mu_v, mu_l, wup, aup, gup,
      row(w0), row(a0), row(k_k), row(k_a), row(r_k), row(lnx_w), row(lnx_b), e1, e2, tri)


def _t5_bucket_table():
    max_exact = N_BUCKETS // 2
    qi = jnp.arange(BLOCK)[:, None]
    kj = jnp.arange(2 * BLOCK)[None, :]
    dist = qi + BLOCK - kj
    in_window = (dist >= 0) & (dist < WINDOW)
    d = jnp.maximum(dist, 0)
    log_ratio = jnp.log(jnp.maximum(d, 1).astype(F32) / max_exact) / math.log(MAX_DISTANCE / max_exact)
    large = max_exact + (log_ratio * (N_BUCKETS - max_exact)).astype(jnp.int32)
    bucket = jnp.where(d < max_exact, d, jnp.minimum(large, N_BUCKETS - 1))
    return jnp.where(in_window, bucket, -1).astype(jnp.int32)


def _bias_body(bucket_ref, rb_ref, o_ref):
    h = pl.program_id(0)
    bucket = bucket_ref[...]
    acc = jnp.zeros(bucket.shape, F32)
    for b in range(N_BUCKETS):
        acc = jnp.where(bucket == b, rb_ref[b, h], acc)
    o_ref[0] = jnp.where(bucket < 0, -jnp.inf, acc)


def _attn_bias(rel_bias):
    nh = rel_bias.shape[1]
    return pl.pallas_call(
        _bias_body,
        grid=(nh,),
        in_specs=[pl.BlockSpec((BLOCK, 2 * BLOCK), lambda h: (0, 0)),
                  pl.BlockSpec(memory_space=pltpu.SMEM)],
        out_specs=pl.BlockSpec((1, BLOCK, 2 * BLOCK), lambda h: (h, 0, 0)),
        out_shape=jax.ShapeDtypeStruct((nh, BLOCK, 2 * BLOCK), F32),
        compiler_params=_cparams(("arbitrary",)),
        name="attn_bias",
    )(_t5_bucket_table(), rel_bias)


def _attn_body(q_ref, k_ref, v_ref, kh_ref, vh_ref, bias_ref, sink_ref, o_ref, kb, vb, kbs, vbs, *, tq, nq, nkv):
    i = pl.program_id(0)
    nblk = tq // BLOCK
    kvw = nkv * HEAD_DIM
    kb[0:BLOCK, :] = kh_ref[...]
    kb[BLOCK:BLOCK + tq, :] = k_ref[...]
    vb[0:BLOCK, :] = vh_ref[...]
    vb[BLOCK:BLOCK + tq, :] = v_ref[...]
    kbs[...] = pltpu.roll(kb[...].astype(F32), HEAD_DIM, axis=1).astype(BF16)
    vbs[...] = pltpu.roll(vb[...].astype(F32), HEAD_DIM, axis=1).astype(BF16)

    lane = lax.broadcasted_iota(jnp.int32, (BLOCK, LANES), 1)
    low_half = lane < HEAD_DIM
    key_in_own_block = lax.broadcasted_iota(jnp.int32, (BLOCK, 2 * BLOCK), 1) >= BLOCK
    scale = HEAD_DIM ** -0.5

    def block_step(b, carry):
        r0 = pl.multiple_of(b * BLOCK, BLOCK)
        has_prev = (i * nblk + b) > 0
        key_ok = jnp.logical_or(has_prev, key_in_own_block)
        kwin = pl.ds(r0, 2 * BLOCK)
        for pair in range(nq // 2):
            g = (2 * pair) // ATT_GROUP
            q2 = q_ref[pl.ds(r0, BLOCK), pair * LANES:(pair + 1) * LANES] * jnp.asarray(scale, BF16)
            halves = []
            for par in range(2):
                h = 2 * pair + par
                if (g % 2) == par:
                    kk, vv, pos = kb, vb, g
                else:
                    kk, vv, pos = kbs, vbs, (g + 1) % nkv
                kv_ln = slice((pos // 2) * LANES, (pos // 2 + 1) * LANES)
                qm = jnp.where(low_half if par == 0 else jnp.logical_not(low_half), q2, jnp.zeros_like(q2))
                logits = _dot_nt(qm, kk[kwin, kv_ln]) + bias_ref[h]
                logits = jnp.where(key_ok, logits, -jnp.inf)
                sink = sink_ref[h]
                mx = jnp.maximum(jnp.max(logits, axis=-1, keepdims=True), sink)
                p = jnp.exp(logits - mx)
                denom = jnp.sum(p, axis=-1, keepdims=True) + jnp.exp(sink - mx)
                halves.append(_dot(p.astype(BF16), vv[kwin, kv_ln]) / denom)
            o_ref[pl.ds(r0, BLOCK), pair * LANES:(pair + 1) * LANES] = (
                jnp.where(low_half, halves[0], halves[1]).astype(o_ref.dtype))
        return carry

    lax.fori_loop(0, nblk, block_step, 0)


def _attention(proj, q_blk, k_blk, v_blk, nq, nkv, bias, sinks, tq):
    s = proj.shape[0]
    qw = nq * HEAD_DIM
    kvw = nkv * HEAD_DIM
    hb = tq // BLOCK
    return pl.pallas_call(
        functools.partial(_attn_body, tq=tq, nq=nq, nkv=nkv),
        grid=(s // tq,),
        in_specs=[pl.BlockSpec((tq, qw), lambda i: (i, q_blk)),
                  pl.BlockSpec((tq, kvw), lambda i: (i, k_blk)),
                  pl.BlockSpec((tq, kvw), lambda i: (i, v_blk)),
                  pl.BlockSpec((BLOCK, kvw), lambda i: (jnp.maximum(i * hb - 1, 0), k_blk)),
                  pl.BlockSpec((BLOCK, kvw), lambda i: (jnp.maximum(i * hb - 1, 0), v_blk)),
                  pl.BlockSpec((nq, BLOCK, 2 * BLOCK), lambda i: (0, 0, 0)),
                  pl.BlockSpec(memory_space=pltpu.SMEM)],
        out_specs=pl.BlockSpec((tq, qw), lambda i: (i, 0)),
        out_shape=jax.ShapeDtypeStruct((s, qw), BF16),
        scratch_shapes=[pltpu.VMEM((tq + BLOCK, kvw), BF16), pltpu.VMEM((tq + BLOCK, kvw), BF16),
                        pltpu.VMEM((tq + BLOCK, kvw), BF16), pltpu.VMEM((tq + BLOCK, kvw), BF16)],
        compiler_params=_cparams(("parallel",)),
        name="swa",
    )(proj, proj, proj, proj, proj, bias, sinks)


def _tile(n, pref):
    t = min(n, pref)
    assert n % t == 0
    return t


def kernel(x, mix_norm_w, w_in, conv_dw_w, conv_dw_b, conv_ln_w, conv_ln_b, conv_pw_w, conv_pw_b, rwkv_mu, rwkv_w_up, rwkv_w0, rwkv_a_up, rwkv_a0, rwkv_g_up, rwkv_k_k, rwkv_k_a, rwkv_r_k, rwkv_lnx_w, rwkv_lnx_b, attn_sinks, rel_bias, w_out, ffn_norm_w, ffn_w_in, ffn_conv_w, ffn_conv_b, ffn_w_out, final_norm_w):
    bsz, s, d = x.shape
    assert bsz == 1
    depth = w_in.shape[0]
    cc = conv_dw_w.shape[2]
    rd = rwkv_w0.shape[1]
    nq = attn_sinks.shape[1]
    nkv = nq // ATT_GROUP
    ad, akv = nq * HEAD_DIM, nkv * HEAD_DIM
    nl = W_LORA + A_LORA + G_LORA
    dff = ffn_w_out.shape[1]
    assert cc == rd and ad == 2 * cc and 4 * akv == cc and w_in.shape[2] == 2 * cc + 3 * rd + nl + ad + 2 * akv
    assert s % 512 == 0

    o_conv, o_rwkv, o_att = 0, 2 * cc, 2 * cc + 3 * rd + nl
    n_proj = ad + 2 * cc + 3 * rd + 2 * akv + LORA_PAD
    n_proj_pad = -(-n_proj // 1024) * 1024
    col_q = 0
    col_val, col_gate = ad // cc, ad // cc + 1
    col_r, col_k, col_v = ad // cc + 2, ad // cc + 3, ad // cc + 4
    off_kv = ad + 2 * cc + 3 * rd
    col_ak, col_av = off_kv // akv, off_kv // akv + 1
    off_l = off_kv + 2 * akv
    assert off_l % LORA_PAD == 0
    col_l = off_l // LORA_PAD

    dff_pad = -(-dff // 1024) * 1024

    def pad_ff(a):
        halves = a.reshape(a.shape[:-1] + (2, dff))
        halves = jnp.pad(halves, [(0, 0)] * (a.ndim - 1) + [(0, 0), (0, dff_pad - dff)])
        return halves.reshape(a.shape[:-1] + (2 * dff_pad,))

    w_in_g = w_in * mix_norm_w[:, :, None]
    w_in_b = jnp.concatenate(
        [w_in_g[:, :, o_att:o_att + ad], w_in_g[:, :, o_conv:o_conv + 2 * cc],
         w_in_g[:, :, o_rwkv:o_rwkv + 3 * rd], w_in_g[:, :, o_att + ad:o_att + ad + 2 * akv],
         w_in_g[:, :, o_rwkv + 3 * rd:o_rwkv + 3 * rd + nl],
         jnp.zeros((depth, d, n_proj_pad - n_proj + LORA_PAD - nl), w_in.dtype)], axis=2).astype(BF16)
    w_out_b = w_out.astype(BF16)
    ffn_w_in_b = _cast_pad_halves(ffn_w_in, ffn_norm_w, dff, dff_pad)
    ffn_w_out_b = _cast_pad_rows(ffn_w_out, dff_pad)
    ffn_conv_w_p = pad_ff(ffn_conv_w)
    ffn_conv_b_p = pad_ff(ffn_conv_b)

    tm = _tile(s, 1024)
    bias = _attn_bias(rel_bias)
    h = x.reshape(s, d)
    hb, ssq = _h_prep(h, _tile(s, 512))
    for l in range(depth):
        proj = _matmul(hb, ssq, w_in_b, l, BF16, tm, 1024)
        y_conv = _conformer(proj, col_val, col_gate, conv_dw_w[l], conv_dw_b[l], conv_ln_w[l], conv_ln_b[l],
                            conv_pw_w[l], conv_pw_b[l], _tile(s, 256))
        y_rwkv = _rwkv(proj, col_r, col_k, col_v, col_l, rwkv_mu[l], rwkv_w_up[l], rwkv_w0[l], rwkv_a_up[l],
                       rwkv_a0[l], rwkv_g_up[l], rwkv_k_k[l], rwkv_k_a[l], rwkv_r_k[l].reshape(-1),
                       rwkv_lnx_w[l], rwkv_lnx_b[l], _tile(s, 256))
        y_att = _attention(proj, col_q, col_ak, col_av, nq, nkv, bias, attn_sinks[l], _tile(s, 512))
        h, hb, ssq = _mix_out(y_conv, y_rwkv, y_att, w_out_b, l, h, tm, 512)
        gact = _ffn_in(hb, ssq, ffn_w_in_b, l, ffn_conv_w_p[l], ffn_conv_b_p[l], dff_pad, tm, 512)
        h, hb, ssq = _ffn_out(gact, ffn_w_out_b, l, h, tm, 1024, dff_pad // 4)
    out = _rmsnorm(h, final_norm_w, F32, _tile(s, 512))
    return out.reshape(bsz, s, d)
```

```python
import functools
import math

import jax
import jax.numpy as jnp
from jax import lax
from jax.experimental import pallas as pl
from jax.experimental.pallas import tpu as pltpu

F32 = jnp.float32
BF16 = jnp.bfloat16

HEAD_DIM = 64
CONV_WIDTH = 31
W_LORA = 64
A_LORA = 64
G_LORA = 160
RWKV_LNX_EPS = 64e-5
ATT_GROUP = 8
WINDOW = 128
BLOCK = 128
N_BUCKETS = 32
MAX_DISTANCE = 128
FFN_CONV_WIDTH = 3
RMS_EPS = 1e-6
LN_EPS = 1e-5

LANES = 128
LORA_PAD = 384
RWKV_CHUNK = 64
HEADS_PER_GROUP = 4
GROUP_LANES = HEADS_PER_GROUP * HEAD_DIM
CONV_HALO = 32
FFN_HALO = 8
CAST_ROWS = 256
VMEM_LIMIT = 56 * 1024 * 1024


def _cparams(sem):
    return pltpu.CompilerParams(dimension_semantics=sem, vmem_limit_bytes=VMEM_LIMIT)


def _dot(a, b):
    return jnp.dot(a, b, preferred_element_type=F32)


def _dot_nt(a, b):
    return lax.dot_general(a, b, (((1,), (1,)), ((), ())), preferred_element_type=F32)


def _split2(x):
    hi = x.astype(BF16)
    lo = (x - hi.astype(F32)).astype(BF16)
    return hi, lo


def _split3(x):
    hi = x.astype(BF16)
    r1 = x - hi.astype(F32)
    mid = r1.astype(BF16)
    lo = (r1 - mid.astype(F32)).astype(BF16)
    return hi, mid, lo


def _rmsnorm_body(x_ref, w_ref, o_ref):
    x = x_ref[...]
    y = x * lax.rsqrt(jnp.mean(x * x, axis=-1, keepdims=True) + RMS_EPS)
    o_ref[...] = (y * w_ref[...]).astype(o_ref.dtype)


def _rmsnorm(x, w, out_dtype, tm):
    s, d = x.shape
    return pl.pallas_call(
        _rmsnorm_body,
        grid=(s // tm,),
        in_specs=[pl.BlockSpec((tm, d), lambda i: (i, 0)),
                  pl.BlockSpec((1, d), lambda i: (0, 0))],
        out_specs=pl.BlockSpec((tm, d), lambda i: (i, 0)),
        out_shape=jax.ShapeDtypeStruct((s, d), out_dtype),
        compiler_params=_cparams(("parallel",)),
        name="rmsnorm",
    )(x, w.reshape(1, d))


def _row_scale(ssq_ref, rows, d):
    return lax.rsqrt(ssq_ref[rows, 0:1] * (1.0 / d) + RMS_EPS)


def _emit_h(hn, col_block, o_ref, ob_ref, ssq_ref):
    o_ref[...] = hn
    ob_ref[...] = hn.astype(ob_ref.dtype)
    part = jnp.broadcast_to(jnp.sum(hn * hn, axis=-1, keepdims=True), ssq_ref.shape)

    @pl.when(col_block == 0)
    def _():
        ssq_ref[...] = part

    @pl.when(col_block > 0)
    def _():
        ssq_ref[...] += part


def _h_prep_body(x_ref, ob_ref, ssq_ref):
    x = x_ref[...]
    ob_ref[...] = x.astype(ob_ref.dtype)
    ssq_ref[...] = jnp.broadcast_to(jnp.sum(x * x, axis=-1, keepdims=True), ssq_ref.shape)


def _h_prep(x, tm):
    s, d = x.shape
    return pl.pallas_call(
        _h_prep_body,
        grid=(s // tm,),
        in_specs=[pl.BlockSpec((tm, d), lambda i: (i, 0))],
        out_specs=[pl.BlockSpec((tm, d), lambda i: (i, 0)), pl.BlockSpec((tm, LANES), lambda i: (i, 0))],
        out_shape=[jax.ShapeDtypeStruct((s, d), BF16), jax.ShapeDtypeStruct((s, LANES), F32)],
        compiler_params=_cparams(("parallel",)),
        name="h_prep",
    )(x)


def _cast_pad_cols_body(x_ref, g_ref, o_ref, *, n):
    o_ref[:, 0:n] = (x_ref[...] * g_ref[...]).astype(o_ref.dtype)
    o_ref[:, n:] = jnp.zeros((o_ref.shape[0], o_ref.shape[1] - n), o_ref.dtype)


def _cast_pad_halves(w, row_gain, n, n_pad):
    nl, r, _ = w.shape
    assert n % LANES == 0 and n_pad % LANES == 0 and r % CAST_ROWS == 0
    return pl.pallas_call(
        functools.partial(_cast_pad_cols_body, n=n),
        grid=(nl, r // CAST_ROWS, 2),
        in_specs=[pl.BlockSpec((None, CAST_ROWS, n), lambda l, i, hf: (l, i, hf)),
                  pl.BlockSpec((None, CAST_ROWS, 1), lambda l, i, hf: (l, i, 0))],
        out_specs=pl.BlockSpec((None, CAST_ROWS, n_pad), lambda l, i, hf: (l, i, hf)),
        out_shape=jax.ShapeDtypeStruct((nl, r, 2 * n_pad), BF16),
        compiler_params=_cparams(("parallel", "parallel", "parallel")),
        name="cast_pad_cols",
    )(w, row_gain.reshape(nl, r, 1))


def _w_in_prep_body(a_ref, b_ref, g_ref, o_ref, *, n_a, n_main, shift, n_q, n_kv, n_l):
    gain = g_ref[...]
    a = a_ref[...] * gain
    b = b_ref[...] * gain
    dt = o_ref.dtype
    n_la = n_a - n_main
    o_ref[:, 0:n_q] = b[:, shift:shift + n_q].astype(dt)
    o_ref[:, n_q:n_q + n_main] = a[:, 0:n_main].astype(dt)
    o_ref[:, n_q + n_main:n_q + n_main + n_kv] = b[:, shift + n_q:shift + n_q + n_kv].astype(dt)
    off = n_q + n_main + n_kv
    o_ref[:, off:off + n_la] = a[:, n_main:n_a].astype(dt)
    lane = lax.broadcasted_iota(jnp.int32, (b.shape[0], LANES), 1)
    o_ref[:, off + n_la:off + n_la + LANES] = jnp.where(lane < n_l - n_la, b[:, 0:LANES], 0.0).astype(dt)
    rest = o_ref.shape[1] - (off + n_la + LANES)
    o_ref[:, off + n_la + LANES:] = jnp.zeros((b.shape[0], rest), dt)


def _w_in_prep(w_in, row_gain, n_main, n_l, n_q, n_kv, n_out):
    nl, r, c = w_in.shape
    q0 = n_main + n_l
    shift = q0 % LANES
    n_a = q0 - shift
    n_b = -(-(c - n_a) // LANES) * LANES
    assert c == q0 + n_q + n_kv and n_a % n_b == 0 and n_main % LANES == 0 and 0 < n_l - (n_a - n_main) <= LANES
    assert n_out >= n_q + n_main + n_kv + (n_a - n_main) + LANES and r % CAST_ROWS == 0
    return pl.pallas_call(
        functools.partial(_w_in_prep_body, n_a=n_a, n_main=n_main, shift=shift, n_q=n_q, n_kv=n_kv, n_l=n_l),
        grid=(nl, r // CAST_ROWS),
        in_specs=[pl.BlockSpec((None, CAST_ROWS, n_a), lambda l, i: (l, i, 0)),
                  pl.BlockSpec((None, CAST_ROWS, n_b), lambda l, i: (l, i, n_a // n_b)),
                  pl.BlockSpec((None, CAST_ROWS, 1), lambda l, i: (l, i, 0))],
        out_specs=pl.BlockSpec((None, CAST_ROWS, n_out), lambda l, i: (l, i, 0)),
        out_shape=jax.ShapeDtypeStruct((nl, r, n_out), BF16),
        compiler_params=_cparams(("parallel", "parallel")),
        name="w_in_prep",
    )(w_in, w_in, row_gain.reshape(nl, r, 1))


def _cast_pad_rows_body(x_ref, o_ref, *, n_valid):
    i = pl.program_id(1)

    @pl.when(i < n_valid)
    def _():
        o_ref[...] = x_ref[...].astype(o_ref.dtype)

    @pl.when(i >= n_valid)
    def _():
        o_ref[...] = jnp.zeros(o_ref.shape, o_ref.dtype)


def _cast_pad_rows(w, r_pad):
    nl, r, c = w.shape
    assert r % CAST_ROWS == 0 and r_pad % CAST_ROWS == 0
    n_valid = r // CAST_ROWS
    return pl.pallas_call(
        functools.partial(_cast_pad_rows_body, n_valid=n_valid),
        grid=(nl, r_pad // CAST_ROWS),
        in_specs=[pl.BlockSpec((None, CAST_ROWS, c), lambda l, i: (l, jnp.minimum(i, n_valid - 1), 0))],
        out_specs=pl.BlockSpec((None, CAST_ROWS, c), lambda l, i: (l, i, 0)),
        out_shape=jax.ShapeDtypeStruct((nl, r_pad, c), BF16),
        compiler_params=_cparams(("parallel", "parallel")),
        name="cast_pad_rows",
    )(w)


def _mm_body(x_ref, w_ref, ssq_ref, o_ref):
    scale = _row_scale(ssq_ref, slice(None), x_ref.shape[1])
    o_ref[...] = (_dot(x_ref[...], w_ref[...]) * scale).astype(o_ref.dtype)


def _matmul(x, ssq, w, layer, out_dtype, tm, tn):
    m, k = x.shape
    n = w.shape[2]
    return pl.pallas_call(
        _mm_body,
        grid=(m // tm, n // tn),
        in_specs=[pl.BlockSpec((tm, k), lambda i, j: (i, 0)),
                  pl.BlockSpec((None, k, tn), lambda i, j: (layer, 0, j)),
                  pl.BlockSpec((tm, ssq.shape[1]), lambda i, j: (i, 0))],
        out_specs=pl.BlockSpec((tm, tn), lambda i, j: (i, j)),
        out_shape=jax.ShapeDtypeStruct((m, n), out_dtype),
        compiler_params=_cparams(("parallel", "arbitrary")),
        name="in_proj",
    )(x, w, ssq)


def _h_outputs(m, n, tm, tn, index, row_index):
    specs = [pl.BlockSpec((tm, tn), index), pl.BlockSpec((tm, tn), index), pl.BlockSpec((tm, LANES), row_index)]
    shapes = [jax.ShapeDtypeStruct((m, n), F32), jax.ShapeDtypeStruct((m, n), BF16),
              jax.ShapeDtypeStruct((m, LANES), F32)]
    return specs, shapes


def _mix_out_body(yc_ref, yr_ref, ya_ref, wc_ref, wr_ref, wa_ref, h_ref, o_ref, ob_ref, ssq_ref):
    acc = _dot(yc_ref[...], wc_ref[...])
    acc += _dot(yr_ref[...], wr_ref[...])
    acc += _dot(ya_ref[...], wa_ref[...])
    _emit_h(h_ref[...] + acc, pl.program_id(1), o_ref, ob_ref, ssq_ref)


def _mix_out(y_conv, y_rwkv, y_att, w_out, layer, h, tm, tn):
    m, kc = y_conv.shape
    kr = y_rwkv.shape[1]
    ka = y_att.shape[1]
    assert kc == kr and ka == 2 * kc
    n = w_out.shape[2]
    out_specs, out_shape = _h_outputs(m, n, tm, tn, lambda i, j: (i, j), lambda i, j: (i, 0))
    return pl.pallas_call(
        _mix_out_body,
        grid=(m // tm, n // tn),
        in_specs=[pl.BlockSpec((tm, kc), lambda i, j: (i, 0)),
                  pl.BlockSpec((tm, kr), lambda i, j: (i, 0)),
                  pl.BlockSpec((tm, ka), lambda i, j: (i, 0)),
                  pl.BlockSpec((None, kc, tn), lambda i, j: (layer, 0, j)),
                  pl.BlockSpec((None, kr, tn), lambda i, j: (layer, 1, j)),
                  pl.BlockSpec((None, ka, tn), lambda i, j: (layer, 1, j)),
                  pl.BlockSpec((tm, tn), lambda i, j: (i, j))],
        out_specs=out_specs,
        out_shape=out_shape,
        compiler_params=_cparams(("parallel", "arbitrary")),
        name="mix_out",
    )(y_conv, y_rwkv, y_att, w_out, w_out, w_out, h)


def _ffn_out_body(x_ref, w_ref, h_ref, o_ref, ob_ref, ssq_ref, acc_ref, *, nk):
    k = pl.program_id(2)

    @pl.when(k == 0)
    def _():
        acc_ref[...] = h_ref[...]

    acc_ref[...] += _dot(x_ref[...], w_ref[...])

    @pl.when(k == nk - 1)
    def _():
        _emit_h(acc_ref[...], pl.program_id(1), o_ref, ob_ref, ssq_ref)


def _ffn_out(g, w, layer, h, tm, tn, tk):
    m, kdim = g.shape
    n = w.shape[2]
    nk = kdim // tk
    out_specs, out_shape = _h_outputs(m, n, tm, tn, lambda i, j, k: (i, j), lambda i, j, k: (i, 0))
    return pl.pallas_call(
        functools.partial(_ffn_out_body, nk=nk),
        grid=(m // tm, n // tn, nk),
        in_specs=[pl.BlockSpec((tm, tk), lambda i, j, k: (i, k)),
                  pl.BlockSpec((None, tk, tn), lambda i, j, k: (layer, k, j)),
                  pl.BlockSpec((tm, tn), lambda i, j, k: (i, j))],
        out_specs=out_specs,
        out_shape=out_shape,
        scratch_shapes=[pltpu.VMEM((tm, tn), F32)],
        compiler_params=_cparams(("parallel", "arbitrary", "arbitrary")),
        name="ffn_out",
    )(g, w, h)


def _ffn_in_body(u_ref, ssq_ref, wg_ref, wu_ref, cwg_ref, cwu_ref, cbg_ref, cbu_ref, o_ref, hg_s, hu_s, *, tm):
    m = pl.program_id(1)

    @pl.when(m == 0)
    def _():
        hg_s[0:FFN_HALO, :] = jnp.zeros((FFN_HALO, hg_s.shape[1]), F32)
        hu_s[0:FFN_HALO, :] = jnp.zeros((FFN_HALO, hu_s.shape[1]), F32)

    def conv(s, cw_ref, cb_ref):
        y = cw_ref[0:1, :] * s[FFN_HALO - 2:FFN_HALO - 2 + tm, :]
        y += cw_ref[1:2, :] * s[FFN_HALO - 1:FFN_HALO - 1 + tm, :]
        y += cw_ref[2:3, :] * s[FFN_HALO:FFN_HALO + tm, :]
        return y + cb_ref[...]

    u = u_ref[...]
    scale = _row_scale(ssq_ref, slice(None), u_ref.shape[1])
    hg_s[FFN_HALO:FFN_HALO + tm, :] = _dot(u, wg_ref[...]) * scale
    hu_s[FFN_HALO:FFN_HALO + tm, :] = _dot(u, wu_ref[...]) * scale
    gate = conv(hg_s, cwg_ref, cbg_ref)
    up = conv(hu_s, cwu_ref, cbu_ref)
    o_ref[...] = (gate * jax.nn.sigmoid(gate) * up).astype(o_ref.dtype)
    hg_s[0:FFN_HALO, :] = hg_s[tm:tm + FFN_HALO, :]
    hu_s[0:FFN_HALO, :] = hu_s[tm:tm + FFN_HALO, :]


def _ffn_in(u, ssq, w_in, layer, conv_w, conv_b, dff, tm, tn):
    s, d = u.shape
    nj = dff // tn
    cw = jnp.pad(conv_w, ((0, 8 - FFN_CONV_WIDTH), (0, 0)))
    cb = conv_b.reshape(1, 2 * dff)
    return pl.pallas_call(
        functools.partial(_ffn_in_body, tm=tm),
        grid=(nj, s // tm),
        in_specs=[pl.BlockSpec((tm, d), lambda j, i: (i, 0)),
                  pl.BlockSpec((tm, ssq.shape[1]), lambda j, i: (i, 0)),
                  pl.BlockSpec((None, d, tn), lambda j, i: (layer, 0, j)),
                  pl.BlockSpec((None, d, tn), lambda j, i: (layer, 0, j + nj)),
                  pl.BlockSpec((8, tn), lambda j, i: (0, j)),
                  pl.BlockSpec((8, tn), lambda j, i: (0, j + nj)),
                  pl.BlockSpec((1, tn), lambda j, i: (0, j)),
                  pl.BlockSpec((1, tn), lambda j, i: (0, j + nj))],
        out_specs=pl.BlockSpec((tm, tn), lambda j, i: (i, j)),
        out_shape=jax.ShapeDtypeStruct((s, dff), BF16),
        scratch_shapes=[pltpu.VMEM((tm + FFN_HALO, tn), F32), pltpu.VMEM((tm + FFN_HALO, tn), F32)],
        compiler_params=_cparams(("parallel", "arbitrary")),
        name="ffn_in",
    )(u, ssq, w_in, w_in, cw, cw, cb, cb)


def _conformer_body(val_ref, gate_ref, valh_ref, gateh_ref, dww_ref, dwb_ref, lnw_ref, lnb_ref,
                    pww_ref, pwb_ref, o_ref, hs, ys, hsb, *, tm, ch):
    i = pl.program_id(0)
    halo = valh_ref[...].astype(F32) * jax.nn.sigmoid(gateh_ref[...].astype(F32))
    hs[0:CONV_HALO, :] = jnp.where(i > 0, halo, 0.0)
    hs[CONV_HALO:CONV_HALO + tm, :] = val_ref[...].astype(F32) * jax.nn.sigmoid(gate_ref[...].astype(F32))

    nrow = tm + CONV_HALO
    for b in range(1, 8):
        hsb[b - 1, 8:nrow, :] = hs[8 - b:nrow - b, :]

    rs, ls = 32, 512
    for r0 in range(0, tm, rs):
        for l0 in range(0, ch, ls):
            acc = jnp.broadcast_to(dwb_ref[0:1, l0:l0 + ls], (rs, ls))
            for shift in range(CONV_WIDTH):
                a, b = divmod(shift, 8)
                j = CONV_WIDTH - 1 - shift
                src = hs if b == 0 else hsb.at[b - 1]
                start = CONV_HALO + r0 - 8 * a
                acc = acc + dww_ref[j:j + 1, l0:l0 + ls] * src[start:start + rs, l0:l0 + ls]
            ys[r0:r0 + rs, l0:l0 + ls] = acc

    y = ys[...]
    mu = jnp.mean(y, axis=-1, keepdims=True)
    yc = y - mu
    var = jnp.mean(yc * yc, axis=-1, keepdims=True)
    z = (yc * lax.rsqrt(var + LN_EPS)) * lnw_ref[...] + lnb_ref[...]
    z = z * jax.nn.sigmoid(z)
    o_ref[...] = (_dot(z.astype(BF16), pww_ref[...]) + pwb_ref[...]).astype(o_ref.dtype)


def _conformer(proj, val_blk, gate_blk, dw_w, dw_b, ln_w, ln_b, pw_w, pw_b, tm):
    s = proj.shape[0]
    ch = dw_w.shape[1]
    hb = tm // CONV_HALO
    row = lambda v: v.reshape(1, ch)
    return pl.pallas_call(
        functools.partial(_conformer_body, tm=tm, ch=ch),
        grid=(s // tm,),
        in_specs=[pl.BlockSpec((tm, ch), lambda i: (i, val_blk)),
                  pl.BlockSpec((tm, ch), lambda i: (i, gate_blk)),
                  pl.BlockSpec((CONV_HALO, ch), lambda i: (jnp.maximum(i * hb - 1, 0), val_blk)),
                  pl.BlockSpec((CONV_HALO, ch), lambda i: (jnp.maximum(i * hb - 1, 0), gate_blk)),
                  pl.BlockSpec((32, ch), lambda i: (0, 0)),
                  pl.BlockSpec((1, ch), lambda i: (0, 0)),
                  pl.BlockSpec((1, ch), lambda i: (0, 0)),
                  pl.BlockSpec((1, ch), lambda i: (0, 0)),
                  pl.BlockSpec((ch, ch), lambda i: (0, 0)),
                  pl.BlockSpec((1, ch), lambda i: (0, 0))],
        out_specs=pl.BlockSpec((tm, ch), lambda i: (i, 0)),
        out_shape=jax.ShapeDtypeStruct((s, ch), BF16),
        scratch_shapes=[pltpu.VMEM((tm + CONV_HALO, ch), F32), pltpu.VMEM((tm, ch), F32),
                        pltpu.VMEM((7, tm + CONV_HALO, ch), F32)],
        compiler_params=_cparams(("parallel",)),
        name="conformer",
    )(proj, proj, proj, proj, jnp.pad(dw_w, ((0, 32 - CONV_WIDTH), (0, 0))), row(dw_b), row(ln_w), row(ln_b),
      pw_w.astype(BF16), row(pw_b))


def _rwkv_body(r_ref, k_ref, v_ref, l_ref, rh_ref, kh_ref, vh_ref, lh_ref,
               mur_ref, muk_ref, muv_ref, mul_ref, wup_ref, aup_ref, gup_ref,
               w0_ref, a0_ref, kk_ref, ka_ref, rk_ref, lnw_ref, lnb_ref,
               e1_ref, e2_ref, tri_ref, o_ref,
               sh, shl, rt_s, at_s, kt_s, bt_s, kw_s, bw_s, v_s, dc_s, y_s, g_s, *, tm, dim):
    i = pl.program_id(0)
    nc = tm // RWKV_CHUNK
    c = RWKV_CHUNK

    @pl.when(i == 0)
    def _():
        g_s[...] = jnp.zeros(g_s.shape, F32)

    def shifted_mix(x_ref, xh_ref, mu_ref, buf):
        x = x_ref[...].astype(F32)
        last = xh_ref[15:16, :].astype(F32)
        buf[7:8, :] = jnp.where(i > 0, last, 0.0)
        buf[8:8 + tm, :] = x
        prev = buf[7:7 + tm, :]
        return x + (prev - x) * mu_ref[...]

    def headsum(x):
        hi, lo = _split2(x)
        s = _dot(hi, e1_ref[...]) + _dot(lo, e1_ref[...])
        shi, slo = _split2(s)
        return _dot(shi, e2_ref[...]) + _dot(slo, e2_ref[...])

    r = shifted_mix(r_ref, rh_ref, mur_ref, sh)
    k = shifted_mix(k_ref, kh_ref, muk_ref, sh)
    v = shifted_mix(v_ref, vh_ref, muv_ref, sh)
    lo_ = shifted_mix(l_ref, lh_ref, mul_ref, shl)

    l_wa = lo_[:, 0:LANES]
    dw = _dot(jnp.tanh(l_wa).astype(BF16), wup_ref[...])
    da = _dot(l_wa.astype(BF16), aup_ref[...])
    g = _dot(jax.nn.sigmoid(lo_[:, LANES:]).astype(BF16), gup_ref[...])
    z = -(w0_ref[...] + dw)
    softplus = jnp.maximum(z, 0.0) + jnp.log(1.0 + jnp.exp(-jnp.abs(z)))
    logw = -jnp.exp(-softplus - 0.5)
    a = jax.nn.sigmoid(a0_ref[...] + da)
    kkr = k * kk_ref[...]
    kk = kkr / jnp.maximum(jnp.sqrt(headsum(kkr * kkr)), 1e-12)
    k2 = k * (1.0 + (a - 1.0) * ka_ref[...])
    bonus = headsum(r * k2 * rk_ref[...]) * v
    a_s = -kk
    b_s = kk * a
    v_s[...] = v

    tri = tri_ref[...]
    for ci in range(nc):
        sl = slice(ci * c, (ci + 1) * c)
        lw = logw[sl]
        h3, m3, l3 = _split3(lw)
        li = _dot(tri, h3) + _dot(tri, m3) + _dot(tri, l3)
        lc = li[c - 1:c, :]
        e_neg = jnp.exp(-li)
        e_end = jnp.exp(lc - li)
        rt_s[sl, :] = r[sl] * jnp.exp(li)
        at_s[sl, :] = a_s[sl] * jnp.exp(li - lw)
        kt_s[sl, :] = k2[sl] * e_neg
        bt_s[sl, :] = b_s[sl] * e_neg
        kw_s[sl, :] = k2[sl] * e_end
        bw_s[sl, :] = b_s[sl] * e_end
        dc_s[ci * 8:(ci + 1) * 8, :] = jnp.broadcast_to(jnp.exp(lc), (8, dim))

    gl = GROUP_LANES
    t_idx = lax.broadcasted_iota(jnp.int32, (c, gl), 0)
    s_idx = jnp.bitwise_and(lax.broadcasted_iota(jnp.int32, (c, gl), 1), HEAD_DIM - 1)
    strict = t_idx > s_idx
    incl = t_idx >= s_idx
    eye = (t_idx == s_idx).astype(F32)
    head_shift = HEAD_DIM.bit_length() - 1
    bd_mask = (jnp.right_shift(lax.broadcasted_iota(jnp.int32, (gl, gl), 0), head_shift)
               == jnp.right_shift(lax.broadcasted_iota(jnp.int32, (gl, gl), 1), head_shift))

    def bd(x):
        return jnp.where(bd_mask, jnp.concatenate([x] * HEADS_PER_GROUP, axis=0), 0.0).astype(BF16)

    def fold(x):
        xm = jnp.where(bd_mask, x, 0.0)
        out = xm[0:HEAD_DIM]
        for q in range(1, HEADS_PER_GROUP):
            out = out + xm[q * HEAD_DIM:(q + 1) * HEAD_DIM]
        return out

    ngrp = dim // gl
    probs = [(ci, q) for ci in range(nc) for q in range(ngrp)]

    def rows_of(ci):
        return slice(ci * c, (ci + 1) * c)

    def lanes_of(q):
        return slice(q * gl, (q + 1) * gl)

    n_ab, a_rb, xl, yv = {}, {}, {}, {}
    for pr in probs:
        rw, ln = rows_of(pr[0]), lanes_of(pr[1])
        lhs = jnp.concatenate([at_s[rw, ln], rt_s[rw, ln]], axis=0).astype(BF16)
        pb = _dot_nt(lhs, bd(bt_s[rw, ln]))
        pk = _dot_nt(lhs, bd(kt_s[rw, ln]))
        n_ab[pr] = jnp.where(strict, pb[0:c], 0.0)
        a_rb[pr] = jnp.where(incl, pb[c:2 * c], 0.0).astype(BF16)
        a_k = jnp.concatenate([jnp.where(strict, pk[0:c], 0.0), jnp.where(incl, pk[c:2 * c], 0.0)], axis=0)
        av = _dot(a_k.astype(BF16), bd(v_s[rw, ln]))
        xl[pr] = av[0:c]
        yv[pr] = av[c:2 * c]

    tinv = {pr: eye + n_ab[pr] for pr in probs}
    mpow = {pr: _dot(n_ab[pr].astype(BF16), bd(n_ab[pr])) for pr in probs}
    for it in range(4):
        for pr in probs:
            res = _dot(jnp.concatenate([mpow[pr], tinv[pr]], axis=0).astype(BF16), bd(mpow[pr]))
            mpow[pr] = res[0:c]
            tinv[pr] = tinv[pr] + res[c:2 * c]
    tinv = {pr: (tinv[pr] + _dot(tinv[pr].astype(BF16), bd(mpow[pr]))).astype(BF16) for pr in probs}

    ta, ul = {}, {}
    for pr in probs:
        rw, ln = rows_of(pr[0]), lanes_of(pr[1])
        ta[pr] = _dot(tinv[pr], bd(at_s[rw, ln]))
        ul[pr] = _dot(tinv[pr], bd(xl[pr]))
    ra, yl, bd_q, cc = {}, {}, {}, {}
    for pr in probs:
        rw, ln = rows_of(pr[0]), lanes_of(pr[1])
        ra[pr] = (rt_s[rw, ln] + _dot(a_rb[pr], bd(ta[pr]))).astype(BF16)
        yl[pr] = _dot(a_rb[pr], bd(ul[pr])) + yv[pr]
        bw = bw_s[rw, ln].astype(BF16)
        bd_q[pr] = bd(fold(_dot(ta[pr].T.astype(BF16), bw)))
        uv = jnp.concatenate([ul[pr], v_s[rw, ln]], axis=0)
        bk = jnp.concatenate([bw, kw_s[rw, ln].astype(BF16)], axis=0)
        cc[pr] = fold(_dot(uv.T.astype(BF16), bk))

    state = [g_s[:, lanes_of(q)] for q in range(ngrp)]
    for ci in range(nc):
        for q in range(ngrp):
            pr = (ci, q)
            gg = state[q]
            y_s[rows_of(ci), lanes_of(q)] = _dot_nt(ra[pr], bd(gg)) + yl[pr]
            gq = _dot(jnp.concatenate(_split2(gg), axis=0), bd_q[pr])
            dc = dc_s[ci * 8:ci * 8 + 1, lanes_of(q)]
            state[q] = dc * gg + (gq[0:HEAD_DIM] + gq[HEAD_DIM:2 * HEAD_DIM]) + cc[pr]
    for q in range(ngrp):
        g_s[:, lanes_of(q)] = state[q]

    y = y_s[...]
    inv_n = 1.0 / HEAD_DIM
    mu = headsum(y) * inv_n
    yc = y - mu
    var = headsum(yc * yc) * inv_n
    yn = (yc * lax.rsqrt(var + RWKV_LNX_EPS)) * lnw_ref[...] + lnb_ref[...]
    o_ref[...] = ((yn + bonus) * g).astype(o_ref.dtype)


def _rwkv(proj, blk_r, blk_k, blk_v, blk_l, mu, w_up, w0, a_up, a0, g_up, k_k, k_a, r_k, lnx_w, lnx_b, tm):
    s = proj.shape[0]
    dim = w0.shape[0]
    nh = dim // HEAD_DIM
    hb = tm // 16
    row = lambda x: x.reshape(1, -1)
    mu_r, mu_k, mu_v = row(mu[0:dim]), row(mu[dim:2 * dim]), row(mu[2 * dim:3 * dim])
    nl = W_LORA + A_LORA + G_LORA
    mu_l = jnp.pad(row(mu[3 * dim:]), ((0, 0), (0, LORA_PAD - nl)))
    assert W_LORA + A_LORA == LANES
    wup = jnp.pad(w_up, ((0, A_LORA), (0, 0))).astype(BF16)
    aup = jnp.pad(a_up, ((W_LORA, 0), (0, 0))).astype(BF16)
    gup = jnp.pad(g_up, ((0, LORA_PAD - nl), (0, 0))).astype(BF16)
    head_of_lane = jnp.arange(dim) // HEAD_DIM
    e1 = (head_of_lane[:, None] == jnp.arange(LANES)[None, :]).astype(BF16)
    e2 = e1.T
    tri = jnp.tril(jnp.ones((RWKV_CHUNK, RWKV_CHUNK), BF16))
    nc = tm // RWKV_CHUNK

    def tile(blk, w):
        return pl.BlockSpec((tm, w), lambda i: (i, blk))

    def halo(blk, w):
        return pl.BlockSpec((16, w), lambda i: (jnp.maximum(i * hb - 1, 0), blk))

    def const(shape):
        return pl.BlockSpec(shape, lambda i: (0, 0))

    big = lambda: pltpu.VMEM((tm, dim), F32)
    return pl.pallas_call(
        functools.partial(_rwkv_body, tm=tm, dim=dim),
        grid=(s // tm,),
        in_specs=[tile(blk_r, dim), tile(blk_k, dim), tile(blk_v, dim), tile(blk_l, LORA_PAD),
                  halo(blk_r, dim), halo(blk_k, dim), halo(blk_v, dim), halo(blk_l, LORA_PAD),
                  const((1, dim)), const((1, dim)), const((1, dim)), const((1, LORA_PAD)),
                  const((LANES, dim)), const((LANES, dim)), const((LORA_PAD - LANES, dim)),
                  const((1, dim)), const((1, dim)), const((1, dim)), const((1, dim)), const((1, dim)),
                  const((1, dim)), const((1, dim)),
                  const((dim, LANES)), const((LANES, dim)), const((RWKV_CHUNK, RWKV_CHUNK))],
        out_specs=pl.BlockSpec((tm, dim), lambda i: (i, 0)),
        out_shape=jax.ShapeDtypeStruct((s, dim), BF16),
        scratch_shapes=[pltpu.VMEM((tm + 8, dim), F32), pltpu.VMEM((tm + 8, LORA_PAD), F32),
                        big(), big(), big(), big(), big(), big(), big(),
                        pltpu.VMEM((nc * 8, dim), F32), big(),
                        pltpu.VMEM((HEAD_DIM, dim), F32)],
        compiler_params=_cparams(("arbitrary",)),
        name="rwkv7",
    )(proj, proj, proj, proj, proj, proj, proj, proj,
      mu_r, mu_k, mu_v, mu_l, wup, aup, gup,
      row(w0), row(a0), row(k_k), row(k_a), row(r_k), row(lnx_w), row(lnx_b), e1, e2, tri)


def _t5_bucket_table():
    max_exact = N_BUCKETS // 2
    qi = jnp.arange(BLOCK)[:, None]
    kj = jnp.arange(2 * BLOCK)[None, :]
    dist = qi + BLOCK - kj
    in_window = (dist >= 0) & (dist < WINDOW)
    d = jnp.maximum(dist, 0)
    log_ratio = jnp.log(jnp.maximum(d, 1).astype(F32) / max_exact) / math.log(MAX_DISTANCE / max_exact)
    large = max_exact + (log_ratio * (N_BUCKETS - max_exact)).astype(jnp.int32)
    bucket = jnp.where(d < max_exact, d, jnp.minimum(large, N_BUCKETS - 1))
    return jnp.where(in_window, bucket, -1).astype(jnp.int32)


def _bias_body(bucket_ref, rb_ref, o_ref):
    h = pl.program_id(0)
    bucket = bucket_ref[...]
    acc = jnp.zeros(bucket.shape, F32)
    for b in range(N_BUCKETS):
        acc = jnp.where(bucket == b, rb_ref[b, h], acc)
    o_ref[0] = jnp.where(bucket < 0, -jnp.inf, acc)


def _attn_bias(rel_bias):
    nh = rel_bias.shape[1]
    return pl.pallas_call(
        _bias_body,
        grid=(nh,),
        in_specs=[pl.BlockSpec((BLOCK, 2 * BLOCK), lambda h: (0, 0)),
                  pl.BlockSpec(memory_space=pltpu.SMEM)],
        out_specs=pl.BlockSpec((1, BLOCK, 2 * BLOCK), lambda h: (h, 0, 0)),
        out_shape=jax.ShapeDtypeStruct((nh, BLOCK, 2 * BLOCK), F32),
        compiler_params=_cparams(("arbitrary",)),
        name="attn_bias",
    )(_t5_bucket_table(), rel_bias)


def _attn_body(q_ref, k_ref, v_ref, kh_ref, vh_ref, bias_ref, sink_ref, o_ref, kb, vb, kbs, vbs, *, tq, nq, nkv):
    i = pl.program_id(0)
    nblk = tq // BLOCK
    kvw = nkv * HEAD_DIM
    kb[0:BLOCK, :] = kh_ref[...]
    kb[BLOCK:BLOCK + tq, :] = k_ref[...]
    vb[0:BLOCK, :] = vh_ref[...]
    vb[BLOCK:BLOCK + tq, :] = v_ref[...]
    kbs[...] = pltpu.roll(kb[...].astype(F32), HEAD_DIM, axis=1).astype(BF16)
    vbs[...] = pltpu.roll(vb[...].astype(F32), HEAD_DIM, axis=1).astype(BF16)

    lane = lax.broadcasted_iota(jnp.int32, (BLOCK, LANES), 1)
    low_half = lane < HEAD_DIM
    key_in_own_block = lax.broadcasted_iota(jnp.int32, (BLOCK, 2 * BLOCK), 1) >= BLOCK
    scale = HEAD_DIM ** -0.5

    def block_step(b, carry):
        r0 = pl.multiple_of(b * BLOCK, BLOCK)
        has_prev = (i * nblk + b) > 0
        key_ok = jnp.logical_or(has_prev, key_in_own_block)
        kwin = pl.ds(r0, 2 * BLOCK)
        for pair in range(nq // 2):
            g = (2 * pair) // ATT_GROUP
            q2 = q_ref[pl.ds(r0, BLOCK), pair * LANES:(pair + 1) * LANES] * jnp.asarray(scale, BF16)
            halves = []
            for par in range(2):
                h = 2 * pair + par
                if (g % 2) == par:
                    kk, vv, pos = kb, vb, g
                else:
                    kk, vv, pos = kbs, vbs, (g + 1) % nkv
                kv_ln = slice((pos // 2) * LANES, (pos // 2 + 1) * LANES)
                qm = jnp.where(low_half if par == 0 else jnp.logical_not(low_half), q2, jnp.zeros_like(q2))
                logits = _dot_nt(qm, kk[kwin, kv_ln]) + bias_ref[h]
                logits = jnp.where(key_ok, logits, -jnp.inf)
                sink = sink_ref[h]
                mx = jnp.maximum(jnp.max(logits, axis=-1, keepdims=True), sink)
                p = jnp.exp(logits - mx)
                denom = jnp.sum(p, axis=-1, keepdims=True) + jnp.exp(sink - mx)
                halves.append(_dot(p.astype(BF16), vv[kwin, kv_ln]) / denom)
            o_ref[pl.ds(r0, BLOCK), pair * LANES:(pair + 1) * LANES] = (
                jnp.where(low_half, halves[0], halves[1]).astype(o_ref.dtype))
        return carry

    lax.fori_loop(0, nblk, block_step, 0)


def _attention(proj, q_blk, k_blk, v_blk, nq, nkv, bias, sinks, tq):
    s = proj.shape[0]
    qw = nq * HEAD_DIM
    kvw = nkv * HEAD_DIM
    hb = tq // BLOCK
    return pl.pallas_call(
        functools.partial(_attn_body, tq=tq, nq=nq, nkv=nkv),
        grid=(s // tq,),
        in_specs=[pl.BlockSpec((tq, qw), lambda i: (i, q_blk)),
                  pl.BlockSpec((tq, kvw), lambda i: (i, k_blk)),
                  pl.BlockSpec((tq, kvw), lambda i: (i, v_blk)),
                  pl.BlockSpec((BLOCK, kvw), lambda i: (jnp.maximum(i * hb - 1, 0), k_blk)),
                  pl.BlockSpec((BLOCK, kvw), lambda i: (jnp.maximum(i * hb - 1, 0), v_blk)),
                  pl.BlockSpec((nq, BLOCK, 2 * BLOCK), lambda i: (0, 0, 0)),
                  pl.BlockSpec(memory_space=pltpu.SMEM)],
        out_specs=pl.BlockSpec((tq, qw), lambda i: (i, 0)),
        out_shape=jax.ShapeDtypeStruct((s, qw), BF16),
        scratch_shapes=[pltpu.VMEM((tq + BLOCK, kvw), BF16), pltpu.VMEM((tq + BLOCK, kvw), BF16),
                        pltpu.VMEM((tq + BLOCK, kvw), BF16), pltpu.VMEM((tq + BLOCK, kvw), BF16)],
        compiler_params=_cparams(("parallel",)),
        name="swa",
    )(proj, proj, proj, proj, proj, bias, sinks)


def _tile(n, pref):
    t = min(n, pref)
    assert n % t == 0
    return t


def kernel(x, mix_norm_w, w_in, conv_dw_w, conv_dw_b, conv_ln_w, conv_ln_b, conv_pw_w, conv_pw_b, rwkv_mu, rwkv_w_up, rwkv_w0, rwkv_a_up, rwkv_a0, rwkv_g_up, rwkv_k_k, rwkv_k_a, rwkv_r_k, rwkv_lnx_w, rwkv_lnx_b, attn_sinks, rel_bias, w_out, ffn_norm_w, ffn_w_in, ffn_conv_w, ffn_conv_b, ffn_w_out, final_norm_w):
    bsz, s, d = x.shape
    assert bsz == 1
    depth = w_in.shape[0]
    cc = conv_dw_w.shape[2]
    rd = rwkv_w0.shape[1]
    nq = attn_sinks.shape[1]
    nkv = nq // ATT_GROUP
    ad, akv = nq * HEAD_DIM, nkv * HEAD_DIM
    nl = W_LORA + A_LORA + G_LORA
    dff = ffn_w_out.shape[1]
    assert cc == rd and ad == 2 * cc and 4 * akv == cc and w_in.shape[2] == 2 * cc + 3 * rd + nl + ad + 2 * akv
    assert s % 512 == 0

    n_proj = ad + 2 * cc + 3 * rd + 2 * akv + LORA_PAD
    n_proj_pad = -(-n_proj // 1024) * 1024
    col_q = 0
    col_val, col_gate = ad // cc, ad // cc + 1
    col_r, col_k, col_v = ad // cc + 2, ad // cc + 3, ad // cc + 4
    off_kv = ad + 2 * cc + 3 * rd
    col_ak, col_av = off_kv // akv, off_kv // akv + 1
    off_l = off_kv + 2 * akv
    assert off_l % LORA_PAD == 0
    col_l = off_l // LORA_PAD

    dff_pad = -(-dff // 1024) * 1024

    def pad_ff(a):
        halves = a.reshape(a.shape[:-1] + (2, dff))
        halves = jnp.pad(halves, [(0, 0)] * (a.ndim - 1) + [(0, 0), (0, dff_pad - dff)])
        return halves.reshape(a.shape[:-1] + (2 * dff_pad,))

    w_in_b = _w_in_prep(w_in, mix_norm_w, 2 * cc + 3 * rd, nl, ad, 2 * akv, n_proj_pad)
    w_out_b = w_out.astype(BF16)
    ffn_w_in_b = _cast_pad_halves(ffn_w_in, ffn_norm_w, dff, dff_pad)
    ffn_w_out_b = _cast_pad_rows(ffn_w_out, dff_pad)
    ffn_conv_w_p = pad_ff(ffn_conv_w)
    ffn_conv_b_p = pad_ff(ffn_conv_b)

    tm = _tile(s, 1024)
    bias = _attn_bias(rel_bias)
    h = x.reshape(s, d)
    hb, ssq = _h_prep(h, _tile(s, 512))
    for l in range(depth):
        proj = _matmul(hb, ssq, w_in_b, l, BF16, tm, 1024)
        y_conv = _conformer(proj, col_val, col_gate, conv_dw_w[l], conv_dw_b[l], conv_ln_w[l], conv_ln_b[l],
                            conv_pw_w[l], conv_pw_b[l], _tile(s, 256))
        y_rwkv = _rwkv(proj, col_r, col_k, col_v, col_l, rwkv_mu[l], rwkv_w_up[l], rwkv_w0[l], rwkv_a_up[l],
                       rwkv_a0[l], rwkv_g_up[l], rwkv_k_k[l], rwkv_k_a[l], rwkv_r_k[l].reshape(-1),
                       rwkv_lnx_w[l], rwkv_lnx_b[l], _tile(s, 256))
        y_att = _attention(proj, col_q, col_ak, col_av, nq, nkv, bias, attn_sinks[l], _tile(s, 512))
        h, hb, ssq = _mix_out(y_conv, y_rwkv, y_att, w_out_b, l, h, tm, 512)
        gact = _ffn_in(hb, ssq, ffn_w_in_b, l, ffn_conv_w_p[l], ffn_conv_b_p[l], dff_pad, tm, 512)
        h, hb, ssq = _ffn_out(gact, ffn_w_out_b, l, h, tm, 1024, dff_pad // 4)
    out = _rmsnorm(h, final_norm_w, F32, _tile(s, 512))
    return out.reshape(bsz, s, d)
```

```python
import functools
import math

import jax
import jax.numpy as jnp
from jax import lax
from jax.experimental import pallas as pl
from jax.experimental.pallas import tpu as pltpu

F32 = jnp.float32
BF16 = jnp.bfloat16

HEAD_DIM = 64
CONV_WIDTH = 31
W_LORA = 64
A_LORA = 64
G_LORA = 160
RWKV_LNX_EPS = 64e-5
ATT_GROUP = 8
WINDOW = 128
BLOCK = 128
N_BUCKETS = 32
MAX_DISTANCE = 128
FFN_CONV_WIDTH = 3
RMS_EPS = 1e-6
LN_EPS = 1e-5

LANES = 128
LORA_PAD = 384
RWKV_CHUNK = 64
HEADS_PER_GROUP = 4
GROUP_LANES = HEADS_PER_GROUP * HEAD_DIM
CONV_HALO = 32
FFN_HALO = 8
CAST_ROWS = 256
VMEM_LIMIT = 56 * 1024 * 1024


def _cparams(sem):
    return pltpu.CompilerParams(dimension_semantics=sem, vmem_limit_bytes=VMEM_LIMIT)


def _dot(a, b):
    return jnp.dot(a, b, preferred_element_type=F32)


def _dot_nt(a, b):
    return lax.dot_general(a, b, (((1,), (1,)), ((), ())), preferred_element_type=F32)


def _split2(x):
    hi = x.astype(BF16)
    lo = (x - hi.astype(F32)).astype(BF16)
    return hi, lo


def _split3(x):
    hi = x.astype(BF16)
    r1 = x - hi.astype(F32)
    mid = r1.astype(BF16)
    lo = (r1 - mid.astype(F32)).astype(BF16)
    return hi, mid, lo


def _rmsnorm_body(x_ref, w_ref, o_ref):
    x = x_ref[...]
    y = x * lax.rsqrt(jnp.mean(x * x, axis=-1, keepdims=True) + RMS_EPS)
    o_ref[...] = (y * w_ref[...]).astype(o_ref.dtype)


def _rmsnorm(x, w, out_dtype, tm):
    s, d = x.shape
    return pl.pallas_call(
        _rmsnorm_body,
        grid=(s // tm,),
        in_specs=[pl.BlockSpec((tm, d), lambda i: (i, 0)),
                  pl.BlockSpec((1, d), lambda i: (0, 0))],
        out_specs=pl.BlockSpec((tm, d), lambda i: (i, 0)),
        out_shape=jax.ShapeDtypeStruct((s, d), out_dtype),
        compiler_params=_cparams(("parallel",)),
        name="rmsnorm",
    )(x, w.reshape(1, d))


def _row_scale(ssq_ref, rows, d):
    return lax.rsqrt(ssq_ref[rows, 0:1] * (1.0 / d) + RMS_EPS)


def _emit_h(hn, col_block, o_ref, ob_ref, ssq_ref):
    o_ref[...] = hn
    ob_ref[...] = hn.astype(ob_ref.dtype)
    part = jnp.broadcast_to(jnp.sum(hn * hn, axis=-1, keepdims=True), ssq_ref.shape)

    @pl.when(col_block == 0)
    def _():
        ssq_ref[...] = part

    @pl.when(col_block > 0)
    def _():
        ssq_ref[...] += part


def _h_prep_body(x_ref, ob_ref, ssq_ref):
    x = x_ref[...]
    ob_ref[...] = x.astype(ob_ref.dtype)
    ssq_ref[...] = jnp.broadcast_to(jnp.sum(x * x, axis=-1, keepdims=True), ssq_ref.shape)


def _h_prep(x, tm):
    s, d = x.shape
    return pl.pallas_call(
        _h_prep_body,
        grid=(s // tm,),
        in_specs=[pl.BlockSpec((tm, d), lambda i: (i, 0))],
        out_specs=[pl.BlockSpec((tm, d), lambda i: (i, 0)), pl.BlockSpec((tm, LANES), lambda i: (i, 0))],
        out_shape=[jax.ShapeDtypeStruct((s, d), BF16), jax.ShapeDtypeStruct((s, LANES), F32)],
        compiler_params=_cparams(("parallel",)),
        name="h_prep",
    )(x)


def _cast_pad_cols_body(x_ref, g_ref, o_ref, *, n):
    o_ref[:, 0:n] = (x_ref[...] * g_ref[...]).astype(o_ref.dtype)
    o_ref[:, n:] = jnp.zeros((o_ref.shape[0], o_ref.shape[1] - n), o_ref.dtype)


def _cast_pad_halves(w, row_gain, n, n_pad):
    nl, r, _ = w.shape
    assert n % LANES == 0 and n_pad % LANES == 0 and r % CAST_ROWS == 0
    return pl.pallas_call(
        functools.partial(_cast_pad_cols_body, n=n),
        grid=(nl, r // CAST_ROWS, 2),
        in_specs=[pl.BlockSpec((None, CAST_ROWS, n), lambda l, i, hf: (l, i, hf)),
                  pl.BlockSpec((None, CAST_ROWS, 1), lambda l, i, hf: (l, i, 0))],
        out_specs=pl.BlockSpec((None, CAST_ROWS, n_pad), lambda l, i, hf: (l, i, hf)),
        out_shape=jax.ShapeDtypeStruct((nl, r, 2 * n_pad), BF16),
        compiler_params=_cparams(("parallel", "parallel", "parallel")),
        name="cast_pad_cols",
    )(w, row_gain.reshape(nl, r, 1))


def _w_in_prep_body(x_ref, g_ref, o_ref, *, valid_rows):
    j = pl.program_id(1)
    x = (x_ref[0] * g_ref[...]).astype(o_ref.dtype)
    row = lax.broadcasted_iota(jnp.int32, x.shape, 0)
    n_valid = jnp.int32(0)
    for blk, n in enumerate(valid_rows):
        n_valid = jnp.where(j == blk, n, n_valid)
    o_ref[...] = jnp.where(row < n_valid, x, jnp.zeros_like(x))


def _w_in_prep(w_in_t, col_gain, n_main, n_l, n_q, n_kv, n_out):
    nl, c, k = w_in_t.shape
    assert c == n_main + n_l + n_q + n_kv and n_out % CAST_ROWS == 0
    runs = [(n_main + n_l, n_q), (0, n_main), (n_main + n_l + n_q, n_kv), (n_main, n_l)]
    starts, valid = [], []
    for src, n in runs[:-1]:
        assert n % CAST_ROWS == 0
        starts += [src + r for r in range(0, n, CAST_ROWS)]
        valid += [CAST_ROWS] * (n // CAST_ROWS)
    src, n = runs[-1]
    for r in range(0, n, CAST_ROWS):
        starts.append(src + r)
        valid.append(min(CAST_ROWS, n - r))
    while len(starts) < n_out // CAST_ROWS:
        starts.append(0)
        valid.append(0)
    assert len(starts) == n_out // CAST_ROWS and all(s0 % 8 == 0 and s0 + CAST_ROWS <= c for s0 in starts)

    def src_row(j):
        off = jnp.int32(0)
        for blk, s0 in enumerate(starts):
            off = jnp.where(j == blk, s0, off)
        return pl.multiple_of(off, 8)

    return pl.pallas_call(
        functools.partial(_w_in_prep_body, valid_rows=tuple(valid)),
        grid=(nl, n_out // CAST_ROWS),
        in_specs=[pl.BlockSpec((pl.Element(1), pl.Element(CAST_ROWS), pl.Element(k)),
                               lambda l, j: (l, src_row(j), 0)),
                  pl.BlockSpec((None, 1, k), lambda l, j: (l, 0, 0))],
        out_specs=pl.BlockSpec((None, CAST_ROWS, k), lambda l, j: (l, j, 0)),
        out_shape=jax.ShapeDtypeStruct((nl, n_out, k), BF16),
        compiler_params=_cparams(("parallel", "parallel")),
        name="w_in_prep",
    )(w_in_t, col_gain.reshape(nl, 1, k))


def _cast_pad_rows_body(x_ref, o_ref, *, n_valid):
    i = pl.program_id(1)

    @pl.when(i < n_valid)
    def _():
        o_ref[...] = x_ref[...].astype(o_ref.dtype)

    @pl.when(i >= n_valid)
    def _():
        o_ref[...] = jnp.zeros(o_ref.shape, o_ref.dtype)


def _cast_pad_rows(w, r_pad):
    nl, r, c = w.shape
    assert r % CAST_ROWS == 0 and r_pad % CAST_ROWS == 0
    n_valid = r // CAST_ROWS
    return pl.pallas_call(
        functools.partial(_cast_pad_rows_body, n_valid=n_valid),
        grid=(nl, r_pad // CAST_ROWS),
        in_specs=[pl.BlockSpec((None, CAST_ROWS, c), lambda l, i: (l, jnp.minimum(i, n_valid - 1), 0))],
        out_specs=pl.BlockSpec((None, CAST_ROWS, c), lambda l, i: (l, i, 0)),
        out_shape=jax.ShapeDtypeStruct((nl, r_pad, c), BF16),
        compiler_params=_cparams(("parallel", "parallel")),
        name="cast_pad_rows",
    )(w)


def _mm_body(x_ref, w_ref, ssq_ref, o_ref):
    scale = _row_scale(ssq_ref, slice(None), x_ref.shape[1])
    o_ref[...] = (_dot_nt(x_ref[...], w_ref[...]) * scale).astype(o_ref.dtype)


def _matmul(x, ssq, w, layer, out_dtype, tm, tn):
    m, k = x.shape
    n = w.shape[1]
    return pl.pallas_call(
        _mm_body,
        grid=(m // tm, n // tn),
        in_specs=[pl.BlockSpec((tm, k), lambda i, j: (i, 0)),
                  pl.BlockSpec((None, tn, k), lambda i, j: (layer, j, 0)),
                  pl.BlockSpec((tm, ssq.shape[1]), lambda i, j: (i, 0))],
        out_specs=pl.BlockSpec((tm, tn), lambda i, j: (i, j)),
        out_shape=jax.ShapeDtypeStruct((m, n), out_dtype),
        compiler_params=_cparams(("parallel", "arbitrary")),
        name="in_proj",
    )(x, w, ssq)


def _h_outputs(m, n, tm, tn, index, row_index):
    specs = [pl.BlockSpec((tm, tn), index), pl.BlockSpec((tm, tn), index), pl.BlockSpec((tm, LANES), row_index)]
    shapes = [jax.ShapeDtypeStruct((m, n), F32), jax.ShapeDtypeStruct((m, n), BF16),
              jax.ShapeDtypeStruct((m, LANES), F32)]
    return specs, shapes


def _mix_out_body(yc_ref, yr_ref, ya_ref, wc_ref, wr_ref, wa_ref, h_ref, o_ref, ob_ref, ssq_ref):
    acc = _dot(yc_ref[...], wc_ref[...])
    acc += _dot(yr_ref[...], wr_ref[...])
    acc += _dot(ya_ref[...], wa_ref[...])
    _emit_h(h_ref[...] + acc, pl.program_id(1), o_ref, ob_ref, ssq_ref)


def _mix_out(y_conv, y_rwkv, y_att, w_out, layer, h, tm, tn):
    m, kc = y_conv.shape
    kr = y_rwkv.shape[1]
    ka = y_att.shape[1]
    assert kc == kr and ka == 2 * kc
    n = w_out.shape[2]
    out_specs, out_shape = _h_outputs(m, n, tm, tn, lambda i, j: (i, j), lambda i, j: (i, 0))
    return pl.pallas_call(
        _mix_out_body,
        grid=(m // tm, n // tn),
        in_specs=[pl.BlockSpec((tm, kc), lambda i, j: (i, 0)),
                  pl.BlockSpec((tm, kr), lambda i, j: (i, 0)),
                  pl.BlockSpec((tm, ka), lambda i, j: (i, 0)),
                  pl.BlockSpec((None, kc, tn), lambda i, j: (layer, 0, j)),
                  pl.BlockSpec((None, kr, tn), lambda i, j: (layer, 1, j)),
                  pl.BlockSpec((None, ka, tn), lambda i, j: (layer, 1, j)),
                  pl.BlockSpec((tm, tn), lambda i, j: (i, j))],
        out_specs=out_specs,
        out_shape=out_shape,
        compiler_params=_cparams(("parallel", "arbitrary")),
        name="mix_out",
    )(y_conv, y_rwkv, y_att, w_out, w_out, w_out, h)


def _ffn_out_body(x_ref, w_ref, h_ref, o_ref, ob_ref, ssq_ref, acc_ref, *, nk):
    k = pl.program_id(2)

    @pl.when(k == 0)
    def _():
        acc_ref[...] = h_ref[...]

    acc_ref[...] += _dot(x_ref[...], w_ref[...])

    @pl.when(k == nk - 1)
    def _():
        _emit_h(acc_ref[...], pl.program_id(1), o_ref, ob_ref, ssq_ref)


def _ffn_out(g, w, layer, h, tm, tn, tk):
    m, kdim = g.shape
    n = w.shape[2]
    nk = kdim // tk
    out_specs, out_shape = _h_outputs(m, n, tm, tn, lambda i, j, k: (i, j), lambda i, j, k: (i, 0))
    return pl.pallas_call(
        functools.partial(_ffn_out_body, nk=nk),
        grid=(m // tm, n // tn, nk),
        in_specs=[pl.BlockSpec((tm, tk), lambda i, j, k: (i, k)),
                  pl.BlockSpec((None, tk, tn), lambda i, j, k: (layer, k, j)),
                  pl.BlockSpec((tm, tn), lambda i, j, k: (i, j))],
        out_specs=out_specs,
        out_shape=out_shape,
        scratch_shapes=[pltpu.VMEM((tm, tn), F32)],
        compiler_params=_cparams(("parallel", "arbitrary", "arbitrary")),
        name="ffn_out",
    )(g, w, h)


def _ffn_in_body(u_ref, ssq_ref, wg_ref, wu_ref, cwg_ref, cwu_ref, cbg_ref, cbu_ref, o_ref, hg_s, hu_s, *, tm):
    m = pl.program_id(1)

    @pl.when(m == 0)
    def _():
        hg_s[0:FFN_HALO, :] = jnp.zeros((FFN_HALO, hg_s.shape[1]), F32)
        hu_s[0:FFN_HALO, :] = jnp.zeros((FFN_HALO, hu_s.shape[1]), F32)

    def conv(s, cw_ref, cb_ref):
        y = cw_ref[0:1, :] * s[FFN_HALO - 2:FFN_HALO - 2 + tm, :]
        y += cw_ref[1:2, :] * s[FFN_HALO - 1:FFN_HALO - 1 + tm, :]
        y += cw_ref[2:3, :] * s[FFN_HALO:FFN_HALO + tm, :]
        return y + cb_ref[...]

    u = u_ref[...]
    scale = _row_scale(ssq_ref, slice(None), u_ref.shape[1])
    hg_s[FFN_HALO:FFN_HALO + tm, :] = _dot(u, wg_ref[...]) * scale
    hu_s[FFN_HALO:FFN_HALO + tm, :] = _dot(u, wu_ref[...]) * scale
    gate = conv(hg_s, cwg_ref, cbg_ref)
    up = conv(hu_s, cwu_ref, cbu_ref)
    o_ref[...] = (gate * jax.nn.sigmoid(gate) * up).astype(o_ref.dtype)
    hg_s[0:FFN_HALO, :] = hg_s[tm:tm + FFN_HALO, :]
    hu_s[0:FFN_HALO, :] = hu_s[tm:tm + FFN_HALO, :]


def _ffn_in(u, ssq, w_in, layer, conv_w, conv_b, dff, tm, tn):
    s, d = u.shape
    nj = dff // tn
    cw = jnp.pad(conv_w, ((0, 8 - FFN_CONV_WIDTH), (0, 0)))
    cb = conv_b.reshape(1, 2 * dff)
    return pl.pallas_call(
        functools.partial(_ffn_in_body, tm=tm),
        grid=(nj, s // tm),
        in_specs=[pl.BlockSpec((tm, d), lambda j, i: (i, 0)),
                  pl.BlockSpec((tm, ssq.shape[1]), lambda j, i: (i, 0)),
                  pl.BlockSpec((None, d, tn), lambda j, i: (layer, 0, j)),
                  pl.BlockSpec((None, d, tn), lambda j, i: (layer, 0, j + nj)),
                  pl.BlockSpec((8, tn), lambda j, i: (0, j)),
                  pl.BlockSpec((8, tn), lambda j, i: (0, j + nj)),
                  pl.BlockSpec((1, tn), lambda j, i: (0, j)),
                  pl.BlockSpec((1, tn), lambda j, i: (0, j + nj))],
        out_specs=pl.BlockSpec((tm, tn), lambda j, i: (i, j)),
        out_shape=jax.ShapeDtypeStruct((s, dff), BF16),
        scratch_shapes=[pltpu.VMEM((tm + FFN_HALO, tn), F32), pltpu.VMEM((tm + FFN_HALO, tn), F32)],
        compiler_params=_cparams(("parallel", "arbitrary")),
        name="ffn_in",
    )(u, ssq, w_in, w_in, cw, cw, cb, cb)


def _conformer_body(val_ref, gate_ref, valh_ref, gateh_ref, dww_ref, dwb_ref, lnw_ref, lnb_ref,
                    pww_ref, pwb_ref, o_ref, hs, ys, hsb, *, tm, ch):
    i = pl.program_id(0)
    halo = valh_ref[...].astype(F32) * jax.nn.sigmoid(gateh_ref[...].astype(F32))
    hs[0:CONV_HALO, :] = jnp.where(i > 0, halo, 0.0)
    hs[CONV_HALO:CONV_HALO + tm, :] = val_ref[...].astype(F32) * jax.nn.sigmoid(gate_ref[...].astype(F32))

    nrow = tm + CONV_HALO
    for b in range(1, 8):
        hsb[b - 1, 8:nrow, :] = hs[8 - b:nrow - b, :]

    rs, ls = 32, 512
    for r0 in range(0, tm, rs):
        for l0 in range(0, ch, ls):
            acc = jnp.broadcast_to(dwb_ref[0:1, l0:l0 + ls], (rs, ls))
            for shift in range(CONV_WIDTH):
                a, b = divmod(shift, 8)
                j = CONV_WIDTH - 1 - shift
                src = hs if b == 0 else hsb.at[b - 1]
                start = CONV_HALO + r0 - 8 * a
                acc = acc + dww_ref[j:j + 1, l0:l0 + ls] * src[start:start + rs, l0:l0 + ls]
            ys[r0:r0 + rs, l0:l0 + ls] = acc

    y = ys[...]
    mu = jnp.mean(y, axis=-1, keepdims=True)
    yc = y - mu
    var = jnp.mean(yc * yc, axis=-1, keepdims=True)
    z = (yc * lax.rsqrt(var + LN_EPS)) * lnw_ref[...] + lnb_ref[...]
    z = z * jax.nn.sigmoid(z)
    o_ref[...] = (_dot(z.astype(BF16), pww_ref[...]) + pwb_ref[...]).astype(o_ref.dtype)


def _conformer(proj, val_blk, gate_blk, dw_w, dw_b, ln_w, ln_b, pw_w, pw_b, tm):
    s = proj.shape[0]
    ch = dw_w.shape[1]
    hb = tm // CONV_HALO
    row = lambda v: v.reshape(1, ch)
    return pl.pallas_call(
        functools.partial(_conformer_body, tm=tm, ch=ch),
        grid=(s // tm,),
        in_specs=[pl.BlockSpec((tm, ch), lambda i: (i, val_blk)),
                  pl.BlockSpec((tm, ch), lambda i: (i, gate_blk)),
                  pl.BlockSpec((CONV_HALO, ch), lambda i: (jnp.maximum(i * hb - 1, 0), val_blk)),
                  pl.BlockSpec((CONV_HALO, ch), lambda i: (jnp.maximum(i * hb - 1, 0), gate_blk)),
                  pl.BlockSpec((32, ch), lambda i: (0, 0)),
                  pl.BlockSpec((1, ch), lambda i: (0, 0)),
                  pl.BlockSpec((1, ch), lambda i: (0, 0)),
                  pl.BlockSpec((1, ch), lambda i: (0, 0)),
                  pl.BlockSpec((ch, ch), lambda i: (0, 0)),
                  pl.BlockSpec((1, ch), lambda i: (0, 0))],
        out_specs=pl.BlockSpec((tm, ch), lambda i: (i, 0)),
        out_shape=jax.ShapeDtypeStruct((s, ch), BF16),
        scratch_shapes=[pltpu.VMEM((tm + CONV_HALO, ch), F32), pltpu.VMEM((tm, ch), F32),
                        pltpu.VMEM((7, tm + CONV_HALO, ch), F32)],
        compiler_params=_cparams(("parallel",)),
        name="conformer",
    )(proj, proj, proj, proj, jnp.pad(dw_w, ((0, 32 - CONV_WIDTH), (0, 0))), row(dw_b), row(ln_w), row(ln_b),
      pw_w.astype(BF16), row(pw_b))


def _rwkv_body(r_ref, k_ref, v_ref, l_ref, rh_ref, kh_ref, vh_ref, lh_ref,
               mur_ref, muk_ref, muv_ref, mul_ref, wup_ref, aup_ref, gup_ref,
               w0_ref, a0_ref, kk_ref, ka_ref, rk_ref, lnw_ref, lnb_ref,
               e1_ref, e2_ref, tri_ref, o_ref,
               sh, shl, rt_s, at_s, kt_s, bt_s, kw_s, bw_s, v_s, dc_s, y_s, g_s, *, tm, dim):
    i = pl.program_id(0)
    nc = tm // RWKV_CHUNK
    c = RWKV_CHUNK

    @pl.when(i == 0)
    def _():
        g_s[...] = jnp.zeros(g_s.shape, F32)

    def shifted_mix(x_ref, xh_ref, mu_ref, buf):
        x = x_ref[...].astype(F32)
        last = xh_ref[15:16, :].astype(F32)
        buf[7:8, :] = jnp.where(i > 0, last, 0.0)
        buf[8:8 + tm, :] = x
        prev = buf[7:7 + tm, :]
        return x + (prev - x) * mu_ref[...]

    def headsum(x):
        hi, lo = _split2(x)
        s = _dot(hi, e1_ref[...]) + _dot(lo, e1_ref[...])
        shi, slo = _split2(s)
        return _dot(shi, e2_ref[...]) + _dot(slo, e2_ref[...])

    r = shifted_mix(r_ref, rh_ref, mur_ref, sh)
    k = shifted_mix(k_ref, kh_ref, muk_ref, sh)
    v = shifted_mix(v_ref, vh_ref, muv_ref, sh)
    lo_ = shifted_mix(l_ref, lh_ref, mul_ref, shl)

    l_wa = lo_[:, 0:LANES]
    dw = _dot(jnp.tanh(l_wa).astype(BF16), wup_ref[...])
    da = _dot(l_wa.astype(BF16), aup_ref[...])
    g = _dot(jax.nn.sigmoid(lo_[:, LANES:]).astype(BF16), gup_ref[...])
    z = -(w0_ref[...] + dw)
    softplus = jnp.maximum(z, 0.0) + jnp.log(1.0 + jnp.exp(-jnp.abs(z)))
    logw = -jnp.exp(-softplus - 0.5)
    a = jax.nn.sigmoid(a0_ref[...] + da)
    kkr = k * kk_ref[...]
    kk = kkr / jnp.maximum(jnp.sqrt(headsum(kkr * kkr)), 1e-12)
    k2 = k * (1.0 + (a - 1.0) * ka_ref[...])
    bonus = headsum(r * k2 * rk_ref[...]) * v
    a_s = -kk
    b_s = kk * a
    v_s[...] = v

    tri = tri_ref[...]
    for ci in range(nc):
        sl = slice(ci * c, (ci + 1) * c)
        lw = logw[sl]
        h3, m3, l3 = _split3(lw)
        li = _dot(tri, h3) + _dot(tri, m3) + _dot(tri, l3)
        lc = li[c - 1:c, :]
        e_neg = jnp.exp(-li)
        e_end = jnp.exp(lc - li)
        rt_s[sl, :] = r[sl] * jnp.exp(li)
        at_s[sl, :] = a_s[sl] * jnp.exp(li - lw)
        kt_s[sl, :] = k2[sl] * e_neg
        bt_s[sl, :] = b_s[sl] * e_neg
        kw_s[sl, :] = k2[sl] * e_end
        bw_s[sl, :] = b_s[sl] * e_end
        dc_s[ci * 8:(ci + 1) * 8, :] = jnp.broadcast_to(jnp.exp(lc), (8, dim))

    gl = GROUP_LANES
    t_idx = lax.broadcasted_iota(jnp.int32, (c, gl), 0)
    s_idx = jnp.bitwise_and(lax.broadcasted_iota(jnp.int32, (c, gl), 1), HEAD_DIM - 1)
    strict = t_idx > s_idx
    incl = t_idx >= s_idx
    eye = (t_idx == s_idx).astype(F32)
    head_shift = HEAD_DIM.bit_length() - 1
    bd_mask = (jnp.right_shift(lax.broadcasted_iota(jnp.int32, (gl, gl), 0), head_shift)
               == jnp.right_shift(lax.broadcasted_iota(jnp.int32, (gl, gl), 1), head_shift))

    def bd(x):
        return jnp.where(bd_mask, jnp.concatenate([x] * HEADS_PER_GROUP, axis=0), 0.0).astype(BF16)

    def fold(x):
        xm = jnp.where(bd_mask, x, 0.0)
        out = xm[0:HEAD_DIM]
        for q in range(1, HEADS_PER_GROUP):
            out = out + xm[q * HEAD_DIM:(q + 1) * HEAD_DIM]
        return out

    ngrp = dim // gl
    probs = [(ci, q) for ci in range(nc) for q in range(ngrp)]

    def rows_of(ci):
        return slice(ci * c, (ci + 1) * c)

    def lanes_of(q):
        return slice(q * gl, (q + 1) * gl)

    n_ab, a_rb, xl, yv = {}, {}, {}, {}
    for pr in probs:
        rw, ln = rows_of(pr[0]), lanes_of(pr[1])
        lhs = jnp.concatenate([at_s[rw, ln], rt_s[rw, ln]], axis=0).astype(BF16)
        pb = _dot_nt(lhs, bd(bt_s[rw, ln]))
        pk = _dot_nt(lhs, bd(kt_s[rw, ln]))
        n_ab[pr] = jnp.where(strict, pb[0:c], 0.0)
        a_rb[pr] = jnp.where(incl, pb[c:2 * c], 0.0).astype(BF16)
        a_k = jnp.concatenate([jnp.where(strict, pk[0:c], 0.0), jnp.where(incl, pk[c:2 * c], 0.0)], axis=0)
        av = _dot(a_k.astype(BF16), bd(v_s[rw, ln]))
        xl[pr] = av[0:c]
        yv[pr] = av[c:2 * c]

    tinv = {pr: eye + n_ab[pr] for pr in probs}
    mpow = {pr: _dot(n_ab[pr].astype(BF16), bd(n_ab[pr])) for pr in probs}
    for it in range(4):
        for pr in probs:
            res = _dot(jnp.concatenate([mpow[pr], tinv[pr]], axis=0).astype(BF16), bd(mpow[pr]))
            mpow[pr] = res[0:c]
            tinv[pr] = tinv[pr] + res[c:2 * c]
    tinv = {pr: (tinv[pr] + _dot(tinv[pr].astype(BF16), bd(mpow[pr]))).astype(BF16) for pr in probs}

    ta, ul = {}, {}
    for pr in probs:
        rw, ln = rows_of(pr[0]), lanes_of(pr[1])
        ta[pr] = _dot(tinv[pr], bd(at_s[rw, ln]))
        ul[pr] = _dot(tinv[pr], bd(xl[pr]))
    ra, yl, bd_q, cc = {}, {}, {}, {}
    for pr in probs:
        rw, ln = rows_of(pr[0]), lanes_of(pr[1])
        ra[pr] = (rt_s[rw, ln] + _dot(a_rb[pr], bd(ta[pr]))).astype(BF16)
        yl[pr] = _dot(a_rb[pr], bd(ul[pr])) + yv[pr]
        bw = bw_s[rw, ln].astype(BF16)
        bd_q[pr] = bd(fold(_dot(ta[pr].T.astype(BF16), bw)))
        uv = jnp.concatenate([ul[pr], v_s[rw, ln]], axis=0)
        bk = jnp.concatenate([bw, kw_s[rw, ln].astype(BF16)], axis=0)
        cc[pr] = fold(_dot(uv.T.astype(BF16), bk))

    state = [g_s[:, lanes_of(q)] for q in range(ngrp)]
    for ci in range(nc):
        for q in range(ngrp):
            pr = (ci, q)
            gg = state[q]
            y_s[rows_of(ci), lanes_of(q)] = _dot_nt(ra[pr], bd(gg)) + yl[pr]
            gq = _dot(jnp.concatenate(_split2(gg), axis=0), bd_q[pr])
            dc = dc_s[ci * 8:ci * 8 + 1, lanes_of(q)]
            state[q] = dc * gg + (gq[0:HEAD_DIM] + gq[HEAD_DIM:2 * HEAD_DIM]) + cc[pr]
    for q in range(ngrp):
        g_s[:, lanes_of(q)] = state[q]

    y = y_s[...]
    inv_n = 1.0 / HEAD_DIM
    mu = headsum(y) * inv_n
    yc = y - mu
    var = headsum(yc * yc) * inv_n
    yn = (yc * lax.rsqrt(var + RWKV_LNX_EPS)) * lnw_ref[...] + lnb_ref[...]
    o_ref[...] = ((yn + bonus) * g).astype(o_ref.dtype)


def _rwkv(proj, blk_r, blk_k, blk_v, blk_l, mu, w_up, w0, a_up, a0, g_up, k_k, k_a, r_k, lnx_w, lnx_b, tm):
    s = proj.shape[0]
    dim = w0.shape[0]
    nh = dim // HEAD_DIM
    hb = tm // 16
    row = lambda x: x.reshape(1, -1)
    mu_r, mu_k, mu_v = row(mu[0:dim]), row(mu[dim:2 * dim]), row(mu[2 * dim:3 * dim])
    nl = W_LORA + A_LORA + G_LORA
    mu_l = jnp.pad(row(mu[3 * dim:]), ((0, 0), (0, LORA_PAD - nl)))
    assert W_LORA + A_LORA == LANES
    wup = jnp.pad(w_up, ((0, A_LORA), (0, 0))).astype(BF16)
    aup = jnp.pad(a_up, ((W_LORA, 0), (0, 0))).astype(BF16)
    gup = jnp.pad(g_up, ((0, LORA_PAD - nl), (0, 0))).astype(BF16)
    head_of_lane = jnp.arange(dim) // HEAD_DIM
    e1 = (head_of_lane[:, None] == jnp.arange(LANES)[None, :]).astype(BF16)
    e2 = e1.T
    tri = jnp.tril(jnp.ones((RWKV_CHUNK, RWKV_CHUNK), BF16))
    nc = tm // RWKV_CHUNK

    def tile(blk, w):
        return pl.BlockSpec((tm, w), lambda i: (i, blk))

    def halo(blk, w):
        return pl.BlockSpec((16, w), lambda i: (jnp.maximum(i * hb - 1, 0), blk))

    def const(shape):
        return pl.BlockSpec(shape, lambda i: (0, 0))

    big = lambda: pltpu.VMEM((tm, dim), F32)
    return pl.pallas_call(
        functools.partial(_rwkv_body, tm=tm, dim=dim),
        grid=(s // tm,),
        in_specs=[tile(blk_r, dim), tile(blk_k, dim), tile(blk_v, dim), tile(blk_l, LORA_PAD),
                  halo(blk_r, dim), halo(blk_k, dim), halo(blk_v, dim), halo(blk_l, LORA_PAD),
                  const((1, dim)), const((1, dim)), const((1, dim)), const((1, LORA_PAD)),
                  const((LANES, dim)), const((LANES, dim)), const((LORA_PAD - LANES, dim)),
                  const((1, dim)), const((1, dim)), const((1, dim)), const((1, dim)), const((1, dim)),
                  const((1, dim)), const((1, dim)),
                  const((dim, LANES)), const((LANES, dim)), const((RWKV_CHUNK, RWKV_CHUNK))],
        out_specs=pl.BlockSpec((tm, dim), lambda i: (i, 0)),
        out_shape=jax.ShapeDtypeStruct((s, dim), BF16),
        scratch_shapes=[pltpu.VMEM((tm + 8, dim), F32), pltpu.VMEM((tm + 8, LORA_PAD), F32),
                        big(), big(), big(), big(), big(), big(), big(),
                        pltpu.VMEM((nc * 8, dim), F32), big(),
                        pltpu.VMEM((HEAD_DIM, dim), F32)],
        compiler_params=_cparams(("arbitrary",)),
        name="rwkv7",
    )(proj, proj, proj, proj, proj, proj, proj, proj,
      mu_r, mu_k, mu_v, mu_l, wup, aup, gup,
      row(w0), row(a0), row(k_k), row(k_a), row(r_k), row(lnx_w), row(lnx_b), e1, e2, tri)


def _t5_bucket_table():
    max_exact = N_BUCKETS // 2
    qi = jnp.arange(BLOCK)[:, None]
    kj = jnp.arange(2 * BLOCK)[None, :]
    dist = qi + BLOCK - kj
    in_window = (dist >= 0) & (dist < WINDOW)
    d = jnp.maximum(dist, 0)
    log_ratio = jnp.log(jnp.maximum(d, 1).astype(F32) / max_exact) / math.log(MAX_DISTANCE / max_exact)
    large = max_exact + (log_ratio * (N_BUCKETS - max_exact)).astype(jnp.int32)
    bucket = jnp.where(d < max_exact, d, jnp.minimum(large, N_BUCKETS - 1))
    return jnp.where(in_window, bucket, -1).astype(jnp.int32)


def _bias_body(bucket_ref, rb_ref, o_ref):
    h = pl.program_id(0)
    bucket = bucket_ref[...]
    acc = jnp.zeros(bucket.shape, F32)
    for b in range(N_BUCKETS):
        acc = jnp.where(bucket == b, rb_ref[b, h], acc)
    o_ref[0] = jnp.where(bucket < 0, -jnp.inf, acc)


def _attn_bias(rel_bias):
    nh = rel_bias.shape[1]
    return pl.pallas_call(
        _bias_body,
        grid=(nh,),
        in_specs=[pl.BlockSpec((BLOCK, 2 * BLOCK), lambda h: (0, 0)),
                  pl.BlockSpec(memory_space=pltpu.SMEM)],
        out_specs=pl.BlockSpec((1, BLOCK, 2 * BLOCK), lambda h: (h, 0, 0)),
        out_shape=jax.ShapeDtypeStruct((nh, BLOCK, 2 * BLOCK), F32),
        compiler_params=_cparams(("arbitrary",)),
        name="attn_bias",
    )(_t5_bucket_table(), rel_bias)


def _attn_body(q_ref, k_ref, v_ref, kh_ref, vh_ref, bias_ref, sink_ref, o_ref, kb, vb, kbs, vbs, *, tq, nq, nkv):
    i = pl.program_id(0)
    nblk = tq // BLOCK
    kvw = nkv * HEAD_DIM
    kb[0:BLOCK, :] = kh_ref[...]
    kb[BLOCK:BLOCK + tq, :] = k_ref[...]
    vb[0:BLOCK, :] = vh_ref[...]
    vb[BLOCK:BLOCK + tq, :] = v_ref[...]
    kbs[...] = pltpu.roll(kb[...].astype(F32), HEAD_DIM, axis=1).astype(BF16)
    vbs[...] = pltpu.roll(vb[...].astype(F32), HEAD_DIM, axis=1).astype(BF16)

    lane = lax.broadcasted_iota(jnp.int32, (BLOCK, LANES), 1)
    low_half = lane < HEAD_DIM
    key_in_own_block = lax.broadcasted_iota(jnp.int32, (BLOCK, 2 * BLOCK), 1) >= BLOCK
    scale = HEAD_DIM ** -0.5

    def block_step(b, carry):
        r0 = pl.multiple_of(b * BLOCK, BLOCK)
        has_prev = (i * nblk + b) > 0
        key_ok = jnp.logical_or(has_prev, key_in_own_block)
        kwin = pl.ds(r0, 2 * BLOCK)
        for pair in range(nq // 2):
            g = (2 * pair) // ATT_GROUP
            q2 = q_ref[pl.ds(r0, BLOCK), pair * LANES:(pair + 1) * LANES] * jnp.asarray(scale, BF16)
            halves = []
            for par in range(2):
                h = 2 * pair + par
                if (g % 2) == par:
                    kk, vv, pos = kb, vb, g
                else:
                    kk, vv, pos = kbs, vbs, (g + 1) % nkv
                kv_ln = slice((pos // 2) * LANES, (pos // 2 + 1) * LANES)
                qm = jnp.where(low_half if par == 0 else jnp.logical_not(low_half), q2, jnp.zeros_like(q2))
                logits = _dot_nt(qm, kk[kwin, kv_ln]) + bias_ref[h]
                logits = jnp.where(key_ok, logits, -jnp.inf)
                sink = sink_ref[h]
                mx = jnp.maximum(jnp.max(logits, axis=-1, keepdims=True), sink)
                p = jnp.exp(logits - mx)
                denom = jnp.sum(p, axis=-1, keepdims=True) + jnp.exp(sink - mx)
                halves.append(_dot(p.astype(BF16), vv[kwin, kv_ln]) / denom)
            o_ref[pl.ds(r0, BLOCK), pair * LANES:(pair + 1) * LANES] = (
                jnp.where(low_half, halves[0], halves[1]).astype(o_ref.dtype))
        return carry

    lax.fori_loop(0, nblk, block_step, 0)


def _attention(proj, q_blk, k_blk, v_blk, nq, nkv, bias, sinks, tq):
    s = proj.shape[0]
    qw = nq * HEAD_DIM
    kvw = nkv * HEAD_DIM
    hb = tq // BLOCK
    return pl.pallas_call(
        functools.partial(_attn_body, tq=tq, nq=nq, nkv=nkv),
        grid=(s // tq,),
        in_specs=[pl.BlockSpec((tq, qw), lambda i: (i, q_blk)),
                  pl.BlockSpec((tq, kvw), lambda i: (i, k_blk)),
                  pl.BlockSpec((tq, kvw), lambda i: (i, v_blk)),
                  pl.BlockSpec((BLOCK, kvw), lambda i: (jnp.maximum(i * hb - 1, 0), k_blk)),
                  pl.BlockSpec((BLOCK, kvw), lambda i: (jnp.maximum(i * hb - 1, 0), v_blk)),
                  pl.BlockSpec((nq, BLOCK, 2 * BLOCK), lambda i: (0, 0, 0)),
                  pl.BlockSpec(memory_space=pltpu.SMEM)],
        out_specs=pl.BlockSpec((tq, qw), lambda i: (i, 0)),
        out_shape=jax.ShapeDtypeStruct((s, qw), BF16),
        scratch_shapes=[pltpu.VMEM((tq + BLOCK, kvw), BF16), pltpu.VMEM((tq + BLOCK, kvw), BF16),
                        pltpu.VMEM((tq + BLOCK, kvw), BF16), pltpu.VMEM((tq + BLOCK, kvw), BF16)],
        compiler_params=_cparams(("parallel",)),
        name="swa",
    )(proj, proj, proj, proj, proj, bias, sinks)


def _tile(n, pref):
    t = min(n, pref)
    assert n % t == 0
    return t


def kernel(x, mix_norm_w, w_in, conv_dw_w, conv_dw_b, conv_ln_w, conv_ln_b, conv_pw_w, conv_pw_b, rwkv_mu, rwkv_w_up, rwkv_w0, rwkv_a_up, rwkv_a0, rwkv_g_up, rwkv_k_k, rwkv_k_a, rwkv_r_k, rwkv_lnx_w, rwkv_lnx_b, attn_sinks, rel_bias, w_out, ffn_norm_w, ffn_w_in, ffn_conv_w, ffn_conv_b, ffn_w_out, final_norm_w):
    bsz, s, d = x.shape
    assert bsz == 1
    depth = w_in.shape[0]
    cc = conv_dw_w.shape[2]
    rd = rwkv_w0.shape[1]
    nq = attn_sinks.shape[1]
    nkv = nq // ATT_GROUP
    ad, akv = nq * HEAD_DIM, nkv * HEAD_DIM
    nl = W_LORA + A_LORA + G_LORA
    dff = ffn_w_out.shape[1]
    assert cc == rd and ad == 2 * cc and 4 * akv == cc and w_in.shape[2] == 2 * cc + 3 * rd + nl + ad + 2 * akv
    assert s % 512 == 0

    n_proj = ad + 2 * cc + 3 * rd + 2 * akv + LORA_PAD
    n_proj_pad = -(-n_proj // 1024) * 1024
    col_q = 0
    col_val, col_gate = ad // cc, ad // cc + 1
    col_r, col_k, col_v = ad // cc + 2, ad // cc + 3, ad // cc + 4
    off_kv = ad + 2 * cc + 3 * rd
    col_ak, col_av = off_kv // akv, off_kv // akv + 1
    off_l = off_kv + 2 * akv
    assert off_l % LORA_PAD == 0
    col_l = off_l // LORA_PAD

    dff_pad = -(-dff // 1024) * 1024

    def pad_ff(a):
        halves = a.reshape(a.shape[:-1] + (2, dff))
        halves = jnp.pad(halves, [(0, 0)] * (a.ndim - 1) + [(0, 0), (0, dff_pad - dff)])
        return halves.reshape(a.shape[:-1] + (2 * dff_pad,))

    w_in_b = _w_in_prep(jnp.swapaxes(w_in, 1, 2), mix_norm_w, 2 * cc + 3 * rd, nl, ad, 2 * akv, n_proj_pad)
    w_out_b = w_out.astype(BF16)
    ffn_w_in_b = _cast_pad_halves(ffn_w_in, ffn_norm_w, dff, dff_pad)
    ffn_w_out_b = _cast_pad_rows(ffn_w_out, dff_pad)
    ffn_conv_w_p = pad_ff(ffn_conv_w)
    ffn_conv_b_p = pad_ff(ffn_conv_b)

    tm = _tile(s, 1024)
    bias = _attn_bias(rel_bias)
    h = x.reshape(s, d)
    hb, ssq = _h_prep(h, _tile(s, 512))
    for l in range(depth):
        proj = _matmul(hb, ssq, w_in_b, l, BF16, tm, 1024)
        y_conv = _conformer(proj, col_val, col_gate, conv_dw_w[l], conv_dw_b[l], conv_ln_w[l], conv_ln_b[l],
                            conv_pw_w[l], conv_pw_b[l], _tile(s, 256))
        y_rwkv = _rwkv(proj, col_r, col_k, col_v, col_l, rwkv_mu[l], rwkv_w_up[l], rwkv_w0[l], rwkv_a_up[l],
                       rwkv_a0[l], rwkv_g_up[l], rwkv_k_k[l], rwkv_k_a[l], rwkv_r_k[l].reshape(-1),
                       rwkv_lnx_w[l], rwkv_lnx_b[l], _tile(s, 256))
        y_att = _attention(proj, col_q, col_ak, col_av, nq, nkv, bias, attn_sinks[l], _tile(s, 512))
        h, hb, ssq = _mix_out(y_conv, y_rwkv, y_att, w_out_b, l, h, tm, 512)
        gact = _ffn_in(hb, ssq, ffn_w_in_b, l, ffn_conv_w_p[l], ffn_conv_b_p[l], dff_pad, tm, 512)
        h, hb, ssq = _ffn_out(gact, ffn_w_out_b, l, h, tm, 1024, dff_pad // 4)
    out = _rmsnorm(h, final_norm_w, F32, _tile(s, 512))
    return out.reshape(bsz, s, d)
```

```python
import functools
import math

import jax
import jax.numpy as jnp
from jax import lax
from jax.experimental import pallas as pl
from jax.experimental.pallas import tpu as pltpu

F32 = jnp.float32
BF16 = jnp.bfloat16

HEAD_DIM = 64
CONV_WIDTH = 31
W_LORA = 64
A_LORA = 64
G_LORA = 160
RWKV_LNX_EPS = 64e-5
ATT_GROUP = 8
WINDOW = 128
BLOCK = 128
N_BUCKETS = 32
MAX_DISTANCE = 128
FFN_CONV_WIDTH = 3
RMS_EPS = 1e-6
LN_EPS = 1e-5

LANES = 128
LORA_PAD = 384
RWKV_CHUNK = 64
HEADS_PER_GROUP = 4
GROUP_LANES = HEADS_PER_GROUP * HEAD_DIM
CONV_HALO = 32
FFN_HALO = 8
CAST_ROWS = 256
BF16_SUBLANES = 16
VMEM_LIMIT = 56 * 1024 * 1024


def _cparams(sem):
    return pltpu.CompilerParams(dimension_semantics=sem, vmem_limit_bytes=VMEM_LIMIT)


def _dot(a, b):
    return jnp.dot(a, b, preferred_element_type=F32)


def _dot_nt(a, b):
    return lax.dot_general(a, b, (((1,), (1,)), ((), ())), preferred_element_type=F32)


def _split2(x):
    hi = x.astype(BF16)
    lo = (x - hi.astype(F32)).astype(BF16)
    return hi, lo


def _split3(x):
    hi = x.astype(BF16)
    r1 = x - hi.astype(F32)
    mid = r1.astype(BF16)
    lo = (r1 - mid.astype(F32)).astype(BF16)
    return hi, mid, lo


def _rmsnorm_body(x_ref, w_ref, o_ref):
    x = x_ref[...]
    y = x * lax.rsqrt(jnp.mean(x * x, axis=-1, keepdims=True) + RMS_EPS)
    o_ref[...] = (y * w_ref[...]).astype(o_ref.dtype)


def _rmsnorm(x, w, out_dtype, tm):
    s, d = x.shape
    return pl.pallas_call(
        _rmsnorm_body,
        grid=(s // tm,),
        in_specs=[pl.BlockSpec((tm, d), lambda i: (i, 0)),
                  pl.BlockSpec((1, d), lambda i: (0, 0))],
        out_specs=pl.BlockSpec((tm, d), lambda i: (i, 0)),
        out_shape=jax.ShapeDtypeStruct((s, d), out_dtype),
        compiler_params=_cparams(("parallel",)),
        name="rmsnorm",
    )(x, w.reshape(1, d))


def _row_scale(ssq_ref, rows, d):
    return lax.rsqrt(ssq_ref[rows, 0:1] * (1.0 / d) + RMS_EPS)


def _emit_h(hn, col_block, o_ref, ob_ref, ssq_ref):
    o_ref[...] = hn
    ob_ref[...] = hn.astype(ob_ref.dtype)
    part = jnp.broadcast_to(jnp.sum(hn * hn, axis=-1, keepdims=True), ssq_ref.shape)

    @pl.when(col_block == 0)
    def _():
        ssq_ref[...] = part

    @pl.when(col_block > 0)
    def _():
        ssq_ref[...] += part


def _h_prep_body(x_ref, ob_ref, ssq_ref):
    x = x_ref[...]
    ob_ref[...] = x.astype(ob_ref.dtype)
    ssq_ref[...] = jnp.broadcast_to(jnp.sum(x * x, axis=-1, keepdims=True), ssq_ref.shape)


def _h_prep(x, tm):
    s, d = x.shape
    return pl.pallas_call(
        _h_prep_body,
        grid=(s // tm,),
        in_specs=[pl.BlockSpec((tm, d), lambda i: (i, 0))],
        out_specs=[pl.BlockSpec((tm, d), lambda i: (i, 0)), pl.BlockSpec((tm, LANES), lambda i: (i, 0))],
        out_shape=[jax.ShapeDtypeStruct((s, d), BF16), jax.ShapeDtypeStruct((s, LANES), F32)],
        compiler_params=_cparams(("parallel",)),
        name="h_prep",
    )(x)


def _cast_pad_cols_body(x_ref, g_ref, o_ref, *, n):
    o_ref[:, 0:n] = (x_ref[...] * g_ref[...]).astype(o_ref.dtype)
    o_ref[:, n:] = jnp.zeros((o_ref.shape[0], o_ref.shape[1] - n), o_ref.dtype)


def _cast_pad_halves(w, row_gain, n, n_pad):
    nl, r, _ = w.shape
    assert n % LANES == 0 and n_pad % LANES == 0 and r % CAST_ROWS == 0
    return pl.pallas_call(
        functools.partial(_cast_pad_cols_body, n=n),
        grid=(nl, r // CAST_ROWS, 2),
        in_specs=[pl.BlockSpec((None, CAST_ROWS, n), lambda l, i, hf: (l, i, hf)),
                  pl.BlockSpec((None, CAST_ROWS, 1), lambda l, i, hf: (l, i, 0))],
        out_specs=pl.BlockSpec((None, CAST_ROWS, n_pad), lambda l, i, hf: (l, i, hf)),
        out_shape=jax.ShapeDtypeStruct((nl, r, 2 * n_pad), BF16),
        compiler_params=_cparams(("parallel", "parallel", "parallel")),
        name="cast_pad_cols",
    )(w, row_gain.reshape(nl, r, 1))


def _w_in_prep_body(x_ref, g_ref, o_ref, *, valid_rows):
    j = pl.program_id(1)
    x = (x_ref[0] * g_ref[...]).astype(o_ref.dtype)
    row = lax.broadcasted_iota(jnp.int32, x.shape, 0)
    n_valid = jnp.int32(0)
    for blk, n in enumerate(valid_rows):
        n_valid = jnp.where(j == blk, n, n_valid)
    o_ref[...] = jnp.where(row < n_valid, x, jnp.zeros_like(x))


def _w_in_prep(w_in_t, col_gain, n_main, n_l, n_q, n_kv, n_out):
    nl, c, k = w_in_t.shape
    assert c == n_main + n_l + n_q + n_kv and n_out % CAST_ROWS == 0
    runs = [(n_main + n_l, n_q), (0, n_main), (n_main + n_l + n_q, n_kv), (n_main, n_l)]
    starts, valid = [], []
    for src, n in runs[:-1]:
        assert n % CAST_ROWS == 0
        starts += [src + r for r in range(0, n, CAST_ROWS)]
        valid += [CAST_ROWS] * (n // CAST_ROWS)
    src, n = runs[-1]
    for r in range(0, n, CAST_ROWS):
        starts.append(src + r)
        valid.append(min(CAST_ROWS, n - r))
    while len(starts) < n_out // CAST_ROWS:
        starts.append(0)
        valid.append(0)
    assert len(starts) == n_out // CAST_ROWS and all(s0 % 8 == 0 and s0 + CAST_ROWS <= c for s0 in starts)

    def src_row(j):
        off = jnp.int32(0)
        for blk, s0 in enumerate(starts):
            off = jnp.where(j == blk, s0, off)
        return pl.multiple_of(off, 8)

    return pl.pallas_call(
        functools.partial(_w_in_prep_body, valid_rows=tuple(valid)),
        grid=(nl, n_out // CAST_ROWS),
        in_specs=[pl.BlockSpec((pl.Element(1), pl.Element(CAST_ROWS), pl.Element(k)),
                               lambda l, j: (l, src_row(j), 0)),
                  pl.BlockSpec((None, 1, k), lambda l, j: (l, 0, 0))],
        out_specs=pl.BlockSpec((None, CAST_ROWS, k), lambda l, j: (l, j, 0)),
        out_shape=jax.ShapeDtypeStruct((nl, n_out, k), BF16),
        compiler_params=_cparams(("parallel", "parallel")),
        name="w_in_prep",
    )(w_in_t, col_gain.reshape(nl, 1, k))


def _mm_body(x_ref, w_ref, ssq_ref, c_ref, o_ref, co_ref):
    scale = _row_scale(ssq_ref, slice(None), x_ref.shape[1])
    o_ref[...] = (_dot_nt(x_ref[...], w_ref[...]) * scale).astype(o_ref.dtype)
    co_ref[...] = c_ref[...].astype(co_ref.dtype)


def _matmul(x, ssq, w, layer, cast_src, out_dtype, tm, tn):
    m, k = x.shape
    n = w.shape[1]
    nj = n // tn
    _, r, c = cast_src.shape
    slab = r // ((m // tm) * nj)
    assert slab * (m // tm) * nj == r and slab % BF16_SUBLANES == 0
    return pl.pallas_call(
        _mm_body,
        grid=(m // tm, nj),
        in_specs=[pl.BlockSpec((tm, k), lambda i, j: (i, 0)),
                  pl.BlockSpec((None, tn, k), lambda i, j: (layer, j, 0)),
                  pl.BlockSpec((tm, ssq.shape[1]), lambda i, j: (i, 0)),
                  pl.BlockSpec((None, slab, c), lambda i, j: (layer, i * nj + j, 0))],
        out_specs=[pl.BlockSpec((tm, tn), lambda i, j: (i, j)),
                   pl.BlockSpec((slab, c), lambda i, j: (i * nj + j, 0))],
        out_shape=[jax.ShapeDtypeStruct((m, n), out_dtype), jax.ShapeDtypeStruct((r, c), BF16)],
        compiler_params=_cparams(("parallel", "arbitrary")),
        name="in_proj",
    )(x, w, ssq, cast_src)


def _h_outputs(m, n, tm, tn, index, row_index):
    specs = [pl.BlockSpec((tm, tn), index), pl.BlockSpec((tm, tn), index), pl.BlockSpec((tm, LANES), row_index)]
    shapes = [jax.ShapeDtypeStruct((m, n), F32), jax.ShapeDtypeStruct((m, n), BF16),
              jax.ShapeDtypeStruct((m, LANES), F32)]
    return specs, shapes


def _mix_out_body(yc_ref, yr_ref, ya_ref, wc_ref, wr_ref, wa_ref, h_ref, o_ref, ob_ref, ssq_ref):
    acc = _dot(yc_ref[...], wc_ref[...])
    acc += _dot(yr_ref[...], wr_ref[...])
    acc += _dot(ya_ref[...], wa_ref[...])
    _emit_h(h_ref[...] + acc, pl.program_id(1), o_ref, ob_ref, ssq_ref)


def _mix_out(y_conv, y_rwkv, y_att, w_out, h, tm, tn):
    m, kc = y_conv.shape
    kr = y_rwkv.shape[1]
    ka = y_att.shape[1]
    assert kc == kr and ka == 2 * kc
    n = w_out.shape[1]
    out_specs, out_shape = _h_outputs(m, n, tm, tn, lambda i, j: (i, j), lambda i, j: (i, 0))
    return pl.pallas_call(
        _mix_out_body,
        grid=(m // tm, n // tn),
        in_specs=[pl.BlockSpec((tm, kc), lambda i, j: (i, 0)),
                  pl.BlockSpec((tm, kr), lambda i, j: (i, 0)),
                  pl.BlockSpec((tm, ka), lambda i, j: (i, 0)),
                  pl.BlockSpec((kc, tn), lambda i, j: (0, j)),
                  pl.BlockSpec((kr, tn), lambda i, j: (1, j)),
                  pl.BlockSpec((ka, tn), lambda i, j: (1, j)),
                  pl.BlockSpec((tm, tn), lambda i, j: (i, j))],
        out_specs=out_specs,
        out_shape=out_shape,
        compiler_params=_cparams(("parallel", "arbitrary")),
        name="mix_out",
    )(y_conv, y_rwkv, y_att, w_out, w_out, w_out, h)


def _ffn_out_body(x_ref, w_ref, h_ref, o_ref, ob_ref, ssq_ref, acc_ref, *, nk):
    k = pl.program_id(2)

    @pl.when(k == 0)
    def _():
        acc_ref[...] = h_ref[...]

    acc_ref[...] += _dot(x_ref[...], w_ref[...])

    @pl.when(k == nk - 1)
    def _():
        _emit_h(acc_ref[...], pl.program_id(1), o_ref, ob_ref, ssq_ref)


def _ffn_out(g, w, h, tm, tn, tk):
    m, kdim = g.shape
    n = w.shape[1]
    nk = kdim // tk
    out_specs, out_shape = _h_outputs(m, n, tm, tn, lambda i, j, k: (i, j), lambda i, j, k: (i, 0))
    return pl.pallas_call(
        functools.partial(_ffn_out_body, nk=nk),
        grid=(m // tm, n // tn, nk),
        in_specs=[pl.BlockSpec((tm, tk), lambda i, j, k: (i, k)),
                  pl.BlockSpec((tk, tn), lambda i, j, k: (k, j)),
                  pl.BlockSpec((tm, tn), lambda i, j, k: (i, j))],
        out_specs=out_specs,
        out_shape=out_shape,
        scratch_shapes=[pltpu.VMEM((tm, tn), F32)],
        compiler_params=_cparams(("parallel", "arbitrary", "arbitrary")),
        name="ffn_out",
    )(g, w, h)


def _ffn_in_body(u_ref, ssq_ref, wg_ref, wu_ref, cwg_ref, cwu_ref, cbg_ref, cbu_ref, c_ref, o_ref, co_ref,
                 hg_s, hu_s, *, tm, n_valid_slabs):
    m = pl.program_id(1)
    step = pl.program_id(0) * pl.num_programs(1) + m
    co_ref[...] = jnp.where(step < n_valid_slabs, c_ref[...], 0.0).astype(co_ref.dtype)

    @pl.when(m == 0)
    def _():
        hg_s[0:FFN_HALO, :] = jnp.zeros((FFN_HALO, hg_s.shape[1]), F32)
        hu_s[0:FFN_HALO, :] = jnp.zeros((FFN_HALO, hu_s.shape[1]), F32)

    def conv(s, cw_ref, cb_ref):
        y = cw_ref[0:1, :] * s[FFN_HALO - 2:FFN_HALO - 2 + tm, :]
        y += cw_ref[1:2, :] * s[FFN_HALO - 1:FFN_HALO - 1 + tm, :]
        y += cw_ref[2:3, :] * s[FFN_HALO:FFN_HALO + tm, :]
        return y + cb_ref[...]

    u = u_ref[...]
    scale = _row_scale(ssq_ref, slice(None), u_ref.shape[1])
    hg_s[FFN_HALO:FFN_HALO + tm, :] = _dot(u, wg_ref[...]) * scale
    hu_s[FFN_HALO:FFN_HALO + tm, :] = _dot(u, wu_ref[...]) * scale
    gate = conv(hg_s, cwg_ref, cbg_ref)
    up = conv(hu_s, cwu_ref, cbu_ref)
    o_ref[...] = (gate * jax.nn.sigmoid(gate) * up).astype(o_ref.dtype)
    hg_s[0:FFN_HALO, :] = hg_s[tm:tm + FFN_HALO, :]
    hu_s[0:FFN_HALO, :] = hu_s[tm:tm + FFN_HALO, :]


def _ffn_in(u, ssq, w_in, layer, conv_w, conv_b, cast_src, dff, tm, tn):
    s, d = u.shape
    nj = dff // tn
    cw = jnp.pad(conv_w, ((0, 8 - FFN_CONV_WIDTH), (0, 0)))
    cb = conv_b.reshape(1, 2 * dff)
    ni = s // tm
    _, r, c = cast_src.shape
    slab = dff // (nj * ni)
    assert slab * nj * ni == dff and slab % BF16_SUBLANES == 0 and r % slab == 0 and r <= dff
    n_valid_slabs = r // slab
    return pl.pallas_call(
        functools.partial(_ffn_in_body, tm=tm, n_valid_slabs=n_valid_slabs),
        grid=(nj, ni),
        in_specs=[pl.BlockSpec((tm, d), lambda j, i: (i, 0)),
                  pl.BlockSpec((tm, ssq.shape[1]), lambda j, i: (i, 0)),
                  pl.BlockSpec((None, d, tn), lambda j, i: (layer, 0, j)),
                  pl.BlockSpec((None, d, tn), lambda j, i: (layer, 0, j + nj)),
                  pl.BlockSpec((8, tn), lambda j, i: (0, j)),
                  pl.BlockSpec((8, tn), lambda j, i: (0, j + nj)),
                  pl.BlockSpec((1, tn), lambda j, i: (0, j)),
                  pl.BlockSpec((1, tn), lambda j, i: (0, j + nj)),
                  pl.BlockSpec((None, slab, c),
                               lambda j, i: (layer, jnp.minimum(j * ni + i, n_valid_slabs - 1), 0))],
        out_specs=[pl.BlockSpec((tm, tn), lambda j, i: (i, j)),
                   pl.BlockSpec((slab, c), lambda j, i: (j * ni + i, 0))],
        out_shape=[jax.ShapeDtypeStruct((s, dff), BF16), jax.ShapeDtypeStruct((dff, c), BF16)],
        scratch_shapes=[pltpu.VMEM((tm + FFN_HALO, tn), F32), pltpu.VMEM((tm + FFN_HALO, tn), F32)],
        compiler_params=_cparams(("arbitrary", "arbitrary")),
        name="ffn_in",
    )(u, ssq, w_in, w_in, cw, cw, cb, cb, cast_src)


def _conformer_body(val_ref, gate_ref, valh_ref, gateh_ref, dww_ref, dwb_ref, lnw_ref, lnb_ref,
                    pww_ref, pwb_ref, o_ref, hs, ys, hsb, *, tm, ch):
    i = pl.program_id(0)
    halo = valh_ref[...].astype(F32) * jax.nn.sigmoid(gateh_ref[...].astype(F32))
    hs[0:CONV_HALO, :] = jnp.where(i > 0, halo, 0.0)
    hs[CONV_HALO:CONV_HALO + tm, :] = val_ref[...].astype(F32) * jax.nn.sigmoid(gate_ref[...].astype(F32))

    nrow = tm + CONV_HALO
    for b in range(1, 8):
        hsb[b - 1, 8:nrow, :] = hs[8 - b:nrow - b, :]

    rs, ls = 32, 512
    for r0 in range(0, tm, rs):
        for l0 in range(0, ch, ls):
            acc = jnp.broadcast_to(dwb_ref[0:1, l0:l0 + ls], (rs, ls))
            for shift in range(CONV_WIDTH):
                a, b = divmod(shift, 8)
                j = CONV_WIDTH - 1 - shift
                src = hs if b == 0 else hsb.at[b - 1]
                start = CONV_HALO + r0 - 8 * a
                acc = acc + dww_ref[j:j + 1, l0:l0 + ls] * src[start:start + rs, l0:l0 + ls]
            ys[r0:r0 + rs, l0:l0 + ls] = acc

    y = ys[...]
    mu = jnp.mean(y, axis=-1, keepdims=True)
    yc = y - mu
    var = jnp.mean(yc * yc, axis=-1, keepdims=True)
    z = (yc * lax.rsqrt(var + LN_EPS)) * lnw_ref[...] + lnb_ref[...]
    z = z * jax.nn.sigmoid(z)
    o_ref[...] = (_dot(z.astype(BF16), pww_ref[...]) + pwb_ref[...]).astype(o_ref.dtype)


def _conformer(proj, val_blk, gate_blk, dw_w, dw_b, ln_w, ln_b, pw_w, pw_b, tm):
    s = proj.shape[0]
    ch = dw_w.shape[1]
    hb = tm // CONV_HALO
    row = lambda v: v.reshape(1, ch)
    return pl.pallas_call(
        functools.partial(_conformer_body, tm=tm, ch=ch),
        grid=(s // tm,),
        in_specs=[pl.BlockSpec((tm, ch), lambda i: (i, val_blk)),
                  pl.BlockSpec((tm, ch), lambda i: (i, gate_blk)),
                  pl.BlockSpec((CONV_HALO, ch), lambda i: (jnp.maximum(i * hb - 1, 0), val_blk)),
                  pl.BlockSpec((CONV_HALO, ch), lambda i: (jnp.maximum(i * hb - 1, 0), gate_blk)),
                  pl.BlockSpec((32, ch), lambda i: (0, 0)),
                  pl.BlockSpec((1, ch), lambda i: (0, 0)),
                  pl.BlockSpec((1, ch), lambda i: (0, 0)),
                  pl.BlockSpec((1, ch), lambda i: (0, 0)),
                  pl.BlockSpec((ch, ch), lambda i: (0, 0)),
                  pl.BlockSpec((1, ch), lambda i: (0, 0))],
        out_specs=pl.BlockSpec((tm, ch), lambda i: (i, 0)),
        out_shape=jax.ShapeDtypeStruct((s, ch), BF16),
        scratch_shapes=[pltpu.VMEM((tm + CONV_HALO, ch), F32), pltpu.VMEM((tm, ch), F32),
                        pltpu.VMEM((7, tm + CONV_HALO, ch), F32)],
        compiler_params=_cparams(("parallel",)),
        name="conformer",
    )(proj, proj, proj, proj, jnp.pad(dw_w, ((0, 32 - CONV_WIDTH), (0, 0))), row(dw_b), row(ln_w), row(ln_b),
      pw_w.astype(BF16), row(pw_b))


def _rwkv_body(r_ref, k_ref, v_ref, l_ref, rh_ref, kh_ref, vh_ref, lh_ref,
               mur_ref, muk_ref, muv_ref, mul_ref, wup_ref, aup_ref, gup_ref,
               w0_ref, a0_ref, kk_ref, ka_ref, rk_ref, lnw_ref, lnb_ref,
               e1_ref, e2_ref, tri_ref, o_ref,
               sh, shl, rt_s, at_s, kt_s, bt_s, kw_s, bw_s, v_s, dc_s, y_s, g_s, *, tm, dim):
    i = pl.program_id(0)
    nc = tm // RWKV_CHUNK
    c = RWKV_CHUNK

    @pl.when(i == 0)
    def _():
        g_s[...] = jnp.zeros(g_s.shape, F32)

    def shifted_mix(x_ref, xh_ref, mu_ref, buf):
        x = x_ref[...].astype(F32)
        last = xh_ref[15:16, :].astype(F32)
        buf[7:8, :] = jnp.where(i > 0, last, 0.0)
        buf[8:8 + tm, :] = x
        prev = buf[7:7 + tm, :]
        return x + (prev - x) * mu_ref[...]

    def headsum(x):
        hi, lo = _split2(x)
        s = _dot(hi, e1_ref[...]) + _dot(lo, e1_ref[...])
        shi, slo = _split2(s)
        return _dot(shi, e2_ref[...]) + _dot(slo, e2_ref[...])

    r = shifted_mix(r_ref, rh_ref, mur_ref, sh)
    k = shifted_mix(k_ref, kh_ref, muk_ref, sh)
    v = shifted_mix(v_ref, vh_ref, muv_ref, sh)
    lo_ = shifted_mix(l_ref, lh_ref, mul_ref, shl)

    l_wa = lo_[:, 0:LANES]
    dw = _dot(jnp.tanh(l_wa).astype(BF16), wup_ref[...])
    da = _dot(l_wa.astype(BF16), aup_ref[...])
    g = _dot(jax.nn.sigmoid(lo_[:, LANES:]).astype(BF16), gup_ref[...])
    z = -(w0_ref[...] + dw)
    softplus = jnp.maximum(z, 0.0) + jnp.log(1.0 + jnp.exp(-jnp.abs(z)))
    logw = -jnp.exp(-softplus - 0.5)
    a = jax.nn.sigmoid(a0_ref[...] + da)
    kkr = k * kk_ref[...]
    kk = kkr / jnp.maximum(jnp.sqrt(headsum(kkr * kkr)), 1e-12)
    k2 = k * (1.0 + (a - 1.0) * ka_ref[...])
    bonus = headsum(r * k2 * rk_ref[...]) * v
    a_s = -kk
    b_s = kk * a
    v_s[...] = v

    tri = tri_ref[...]
    for ci in range(nc):
        sl = slice(ci * c, (ci + 1) * c)
        lw = logw[sl]
        h3, m3, l3 = _split3(lw)
        li = _dot(tri, h3) + _dot(tri, m3) + _dot(tri, l3)
        lc = li[c - 1:c, :]
        e_neg = jnp.exp(-li)
        e_end = jnp.exp(lc - li)
        rt_s[sl, :] = r[sl] * jnp.exp(li)
        at_s[sl, :] = a_s[sl] * jnp.exp(li - lw)
        kt_s[sl, :] = k2[sl] * e_neg
        bt_s[sl, :] = b_s[sl] * e_neg
        kw_s[sl, :] = k2[sl] * e_end
        bw_s[sl, :] = b_s[sl] * e_end
        dc_s[ci * 8:(ci + 1) * 8, :] = jnp.broadcast_to(jnp.exp(lc), (8, dim))

    gl = GROUP_LANES
    t_idx = lax.broadcasted_iota(jnp.int32, (c, gl), 0)
    s_idx = jnp.bitwise_and(lax.broadcasted_iota(jnp.int32, (c, gl), 1), HEAD_DIM - 1)
    strict = t_idx > s_idx
    incl = t_idx >= s_idx
    eye = (t_idx == s_idx).astype(F32)
    head_shift = HEAD_DIM.bit_length() - 1
    bd_mask = (jnp.right_shift(lax.broadcasted_iota(jnp.int32, (gl, gl), 0), head_shift)
               == jnp.right_shift(lax.broadcasted_iota(jnp.int32, (gl, gl), 1), head_shift))

    def bd(x):
        return jnp.where(bd_mask, jnp.concatenate([x] * HEADS_PER_GROUP, axis=0), 0.0).astype(BF16)

    def fold(x):
        xm = jnp.where(bd_mask, x, 0.0)
        out = xm[0:HEAD_DIM]
        for q in range(1, HEADS_PER_GROUP):
            out = out + xm[q * HEAD_DIM:(q + 1) * HEAD_DIM]
        return out

    ngrp = dim // gl
    probs = [(ci, q) for ci in range(nc) for q in range(ngrp)]

    def rows_of(ci):
        return slice(ci * c, (ci + 1) * c)

    def lanes_of(q):
        return slice(q * gl, (q + 1) * gl)

    n_ab, a_rb, xl, yv = {}, {}, {}, {}
    for pr in probs:
        rw, ln = rows_of(pr[0]), lanes_of(pr[1])
        lhs = jnp.concatenate([at_s[rw, ln], rt_s[rw, ln]], axis=0).astype(BF16)
        pb = _dot_nt(lhs, bd(bt_s[rw, ln]))
        pk = _dot_nt(lhs, bd(kt_s[rw, ln]))
        n_ab[pr] = jnp.where(strict, pb[0:c], 0.0)
        a_rb[pr] = jnp.where(incl, pb[c:2 * c], 0.0).astype(BF16)
        a_k = jnp.concatenate([jnp.where(strict, pk[0:c], 0.0), jnp.where(incl, pk[c:2 * c], 0.0)], axis=0)
        av = _dot(a_k.astype(BF16), bd(v_s[rw, ln]))
        xl[pr] = av[0:c]
        yv[pr] = av[c:2 * c]

    tinv = {pr: eye + n_ab[pr] for pr in probs}
    mpow = {pr: _dot(n_ab[pr].astype(BF16), bd(n_ab[pr])) for pr in probs}
    for it in range(4):
        for pr in probs:
            res = _dot(jnp.concatenate([mpow[pr], tinv[pr]], axis=0).astype(BF16), bd(mpow[pr]))
            mpow[pr] = res[0:c]
            tinv[pr] = tinv[pr] + res[c:2 * c]
    tinv = {pr: (tinv[pr] + _dot(tinv[pr].astype(BF16), bd(mpow[pr]))).astype(BF16) for pr in probs}

    ta, ul = {}, {}
    for pr in probs:
        rw, ln = rows_of(pr[0]), lanes_of(pr[1])
        ta[pr] = _dot(tinv[pr], bd(at_s[rw, ln]))
        ul[pr] = _dot(tinv[pr], bd(xl[pr]))
    ra, yl, bd_q, cc = {}, {}, {}, {}
    for pr in probs:
        rw, ln = rows_of(pr[0]), lanes_of(pr[1])
        ra[pr] = (rt_s[rw, ln] + _dot(a_rb[pr], bd(ta[pr]))).astype(BF16)
        yl[pr] = _dot(a_rb[pr], bd(ul[pr])) + yv[pr]
        bw = bw_s[rw, ln].astype(BF16)
        bd_q[pr] = bd(fold(_dot(ta[pr].T.astype(BF16), bw)))
        uv = jnp.concatenate([ul[pr], v_s[rw, ln]], axis=0)
        bk = jnp.concatenate([bw, kw_s[rw, ln].astype(BF16)], axis=0)
        cc[pr] = fold(_dot(uv.T.astype(BF16), bk))

    state = [g_s[:, lanes_of(q)] for q in range(ngrp)]
    for ci in range(nc):
        for q in range(ngrp):
            pr = (ci, q)
            gg = state[q]
            y_s[rows_of(ci), lanes_of(q)] = _dot_nt(ra[pr], bd(gg)) + yl[pr]
            gq = _dot(jnp.concatenate(_split2(gg), axis=0), bd_q[pr])
            dc = dc_s[ci * 8:ci * 8 + 1, lanes_of(q)]
            state[q] = dc * gg + (gq[0:HEAD_DIM] + gq[HEAD_DIM:2 * HEAD_DIM]) + cc[pr]
    for q in range(ngrp):
        g_s[:, lanes_of(q)] = state[q]

    y = y_s[...]
    inv_n = 1.0 / HEAD_DIM
    mu = headsum(y) * inv_n
    yc = y - mu
    var = headsum(yc * yc) * inv_n
    yn = (yc * lax.rsqrt(var + RWKV_LNX_EPS)) * lnw_ref[...] + lnb_ref[...]
    o_ref[...] = ((yn + bonus) * g).astype(o_ref.dtype)


def _rwkv(proj, blk_r, blk_k, blk_v, blk_l, mu, w_up, w0, a_up, a0, g_up, k_k, k_a, r_k, lnx_w, lnx_b, tm):
    s = proj.shape[0]
    dim = w0.shape[0]
    nh = dim // HEAD_DIM
    hb = tm // 16
    row = lambda x: x.reshape(1, -1)
    mu_r, mu_k, mu_v = row(mu[0:dim]), row(mu[dim:2 * dim]), row(mu[2 * dim:3 * dim])
    nl = W_LORA + A_LORA + G_LORA
    mu_l = jnp.pad(row(mu[3 * dim:]), ((0, 0), (0, LORA_PAD - nl)))
    assert W_LORA + A_LORA == LANES
    wup = jnp.pad(w_up, ((0, A_LORA), (0, 0))).astype(BF16)
    aup = jnp.pad(a_up, ((W_LORA, 0), (0, 0))).astype(BF16)
    gup = jnp.pad(g_up, ((0, LORA_PAD - nl), (0, 0))).astype(BF16)
    head_of_lane = jnp.arange(dim) // HEAD_DIM
    e1 = (head_of_lane[:, None] == jnp.arange(LANES)[None, :]).astype(BF16)
    e2 = e1.T
    tri = jnp.tril(jnp.ones((RWKV_CHUNK, RWKV_CHUNK), BF16))
    nc = tm // RWKV_CHUNK

    def tile(blk, w):
        return pl.BlockSpec((tm, w), lambda i: (i, blk))

    def halo(blk, w):
        return pl.BlockSpec((16, w), lambda i: (jnp.maximum(i * hb - 1, 0), blk))

    def const(shape):
        return pl.BlockSpec(shape, lambda i: (0, 0))

    big = lambda: pltpu.VMEM((tm, dim), F32)
    return pl.pallas_call(
        functools.partial(_rwkv_body, tm=tm, dim=dim),
        grid=(s // tm,),
        in_specs=[tile(blk_r, dim), tile(blk_k, dim), tile(blk_v, dim), tile(blk_l, LORA_PAD),
                  halo(blk_r, dim), halo(blk_k, dim), halo(blk_v, dim), halo(blk_l, LORA_PAD),
                  const((1, dim)), const((1, dim)), const((1, dim)), const((1, LORA_PAD)),
                  const((LANES, dim)), const((LANES, dim)), const((LORA_PAD - LANES, dim)),
                  const((1, dim)), const((1, dim)), const((1, dim)), const((1, dim)), const((1, dim)),
                  const((1, dim)), const((1, dim)),
                  const((dim, LANES)), const((LANES, dim)), const((RWKV_CHUNK, RWKV_CHUNK))],
        out_specs=pl.BlockSpec((tm, dim), lambda i: (i, 0)),
        out_shape=jax.ShapeDtypeStruct((s, dim), BF16),
        scratch_shapes=[pltpu.VMEM((tm + 8, dim), F32), pltpu.VMEM((tm + 8, LORA_PAD), F32),
                        big(), big(), big(), big(), big(), big(), big(),
                        pltpu.VMEM((nc * 8, dim), F32), big(),
                        pltpu.VMEM((HEAD_DIM, dim), F32)],
        compiler_params=_cparams(("arbitrary",)),
        name="rwkv7",
    )(proj, proj, proj, proj, proj, proj, proj, proj,
      mu_r, mu_k, mu_v, mu_l, wup, aup, gup,
      row(w0), row(a0), row(k_k), row(k_a), row(r_k), row(lnx_w), row(lnx_b), e1, e2, tri)


def _t5_bucket_table():
    max_exact = N_BUCKETS // 2
    qi = jnp.arange(BLOCK)[:, None]
    kj = jnp.arange(2 * BLOCK)[None, :]
    dist = qi + BLOCK - kj
    in_window = (dist >= 0) & (dist < WINDOW)
    d = jnp.maximum(dist, 0)
    log_ratio = jnp.log(jnp.maximum(d, 1).astype(F32) / max_exact) / math.log(MAX_DISTANCE / max_exact)
    large = max_exact + (log_ratio * (N_BUCKETS - max_exact)).astype(jnp.int32)
    bucket = jnp.where(d < max_exact, d, jnp.minimum(large, N_BUCKETS - 1))
    return jnp.where(in_window, bucket, -1).astype(jnp.int32)


def _bias_body(bucket_ref, rb_ref, o_ref):
    h = pl.program_id(0)
    bucket = bucket_ref[...]
    acc = jnp.zeros(bucket.shape, F32)
    for b in range(N_BUCKETS):
        acc = jnp.where(bucket == b, rb_ref[b, h], acc)
    o_ref[0] = jnp.where(bucket < 0, -jnp.inf, acc)


def _attn_bias(rel_bias):
    nh = rel_bias.shape[1]
    return pl.pallas_call(
        _bias_body,
        grid=(nh,),
        in_specs=[pl.BlockSpec((BLOCK, 2 * BLOCK), lambda h: (0, 0)),
                  pl.BlockSpec(memory_space=pltpu.SMEM)],
        out_specs=pl.BlockSpec((1, BLOCK, 2 * BLOCK), lambda h: (h, 0, 0)),
        out_shape=jax.ShapeDtypeStruct((nh, BLOCK, 2 * BLOCK), F32),
        compiler_params=_cparams(("arbitrary",)),
        name="attn_bias",
    )(_t5_bucket_table(), rel_bias)


def _attn_body(q_ref, k_ref, v_ref, kh_ref, vh_ref, bias_ref, sink_ref, o_ref, kb, vb, kbs, vbs, *, tq, nq, nkv):
    i = pl.program_id(0)
    nblk = tq // BLOCK
    kvw = nkv * HEAD_DIM
    kb[0:BLOCK, :] = kh_ref[...]
    kb[BLOCK:BLOCK + tq, :] = k_ref[...]
    vb[0:BLOCK, :] = vh_ref[...]
    vb[BLOCK:BLOCK + tq, :] = v_ref[...]
    kbs[...] = pltpu.roll(kb[...].astype(F32), HEAD_DIM, axis=1).astype(BF16)
    vbs[...] = pltpu.roll(vb[...].astype(F32), HEAD_DIM, axis=1).astype(BF16)

    lane = lax.broadcasted_iota(jnp.int32, (BLOCK, LANES), 1)
    low_half = lane < HEAD_DIM
    key_in_own_block = lax.broadcasted_iota(jnp.int32, (BLOCK, 2 * BLOCK), 1) >= BLOCK
    scale = HEAD_DIM ** -0.5

    def block_step(b, carry):
        r0 = pl.multiple_of(b * BLOCK, BLOCK)
        has_prev = (i * nblk + b) > 0
        key_ok = jnp.logical_or(has_prev, key_in_own_block)
        kwin = pl.ds(r0, 2 * BLOCK)
        for pair in range(nq // 2):
            g = (2 * pair) // ATT_GROUP
            q2 = q_ref[pl.ds(r0, BLOCK), pair * LANES:(pair + 1) * LANES] * jnp.asarray(scale, BF16)
            halves = []
            for par in range(2):
                h = 2 * pair + par
                if (g % 2) == par:
                    kk, vv, pos = kb, vb, g
                else:
                    kk, vv, pos = kbs, vbs, (g + 1) % nkv
                kv_ln = slice((pos // 2) * LANES, (pos // 2 + 1) * LANES)
                qm = jnp.where(low_half if par == 0 else jnp.logical_not(low_half), q2, jnp.zeros_like(q2))
                logits = _dot_nt(qm, kk[kwin, kv_ln]) + bias_ref[h]
                logits = jnp.where(key_ok, logits, -jnp.inf)
                sink = sink_ref[h]
                mx = jnp.maximum(jnp.max(logits, axis=-1, keepdims=True), sink)
                p = jnp.exp(logits - mx)
                denom = jnp.sum(p, axis=-1, keepdims=True) + jnp.exp(sink - mx)
                halves.append(_dot(p.astype(BF16), vv[kwin, kv_ln]) / denom)
            o_ref[pl.ds(r0, BLOCK), pair * LANES:(pair + 1) * LANES] = (
                jnp.where(low_half, halves[0], halves[1]).astype(o_ref.dtype))
        return carry

    lax.fori_loop(0, nblk, block_step, 0)


def _attention(proj, q_blk, k_blk, v_blk, nq, nkv, bias, sinks, tq):
    s = proj.shape[0]
    qw = nq * HEAD_DIM
    kvw = nkv * HEAD_DIM
    hb = tq // BLOCK
    return pl.pallas_call(
        functools.partial(_attn_body, tq=tq, nq=nq, nkv=nkv),
        grid=(s // tq,),
        in_specs=[pl.BlockSpec((tq, qw), lambda i: (i, q_blk)),
                  pl.BlockSpec((tq, kvw), lambda i: (i, k_blk)),
                  pl.BlockSpec((tq, kvw), lambda i: (i, v_blk)),
                  pl.BlockSpec((BLOCK, kvw), lambda i: (jnp.maximum(i * hb - 1, 0), k_blk)),
                  pl.BlockSpec((BLOCK, kvw), lambda i: (jnp.maximum(i * hb - 1, 0), v_blk)),
                  pl.BlockSpec((nq, BLOCK, 2 * BLOCK), lambda i: (0, 0, 0)),
                  pl.BlockSpec(memory_space=pltpu.SMEM)],
        out_specs=pl.BlockSpec((tq, qw), lambda i: (i, 0)),
        out_shape=jax.ShapeDtypeStruct((s, qw), BF16),
        scratch_shapes=[pltpu.VMEM((tq + BLOCK, kvw), BF16), pltpu.VMEM((tq + BLOCK, kvw), BF16),
                        pltpu.VMEM((tq + BLOCK, kvw), BF16), pltpu.VMEM((tq + BLOCK, kvw), BF16)],
        compiler_params=_cparams(("parallel",)),
        name="swa",
    )(proj, proj, proj, proj, proj, bias, sinks)


def _tile(n, pref):
    t = min(n, pref)
    assert n % t == 0
    return t


def kernel(x, mix_norm_w, w_in, conv_dw_w, conv_dw_b, conv_ln_w, conv_ln_b, conv_pw_w, conv_pw_b, rwkv_mu, rwkv_w_up, rwkv_w0, rwkv_a_up, rwkv_a0, rwkv_g_up, rwkv_k_k, rwkv_k_a, rwkv_r_k, rwkv_lnx_w, rwkv_lnx_b, attn_sinks, rel_bias, w_out, ffn_norm_w, ffn_w_in, ffn_conv_w, ffn_conv_b, ffn_w_out, final_norm_w):
    bsz, s, d = x.shape
    assert bsz == 1
    depth = w_in.shape[0]
    cc = conv_dw_w.shape[2]
    rd = rwkv_w0.shape[1]
    nq = attn_sinks.shape[1]
    nkv = nq // ATT_GROUP
    ad, akv = nq * HEAD_DIM, nkv * HEAD_DIM
    nl = W_LORA + A_LORA + G_LORA
    dff = ffn_w_out.shape[1]
    assert cc == rd and ad == 2 * cc and 4 * akv == cc and w_in.shape[2] == 2 * cc + 3 * rd + nl + ad + 2 * akv
    assert s % 512 == 0

    n_proj = ad + 2 * cc + 3 * rd + 2 * akv + LORA_PAD
    n_proj_pad = -(-n_proj // 1024) * 1024
    col_q = 0
    col_val, col_gate = ad // cc, ad // cc + 1
    col_r, col_k, col_v = ad // cc + 2, ad // cc + 3, ad // cc + 4
    off_kv = ad + 2 * cc + 3 * rd
    col_ak, col_av = off_kv // akv, off_kv // akv + 1
    off_l = off_kv + 2 * akv
    assert off_l % LORA_PAD == 0
    col_l = off_l // LORA_PAD

    dff_pad = -(-dff // 1024) * 1024

    def pad_ff(a):
        halves = a.reshape(a.shape[:-1] + (2, dff))
        halves = jnp.pad(halves, [(0, 0)] * (a.ndim - 1) + [(0, 0), (0, dff_pad - dff)])
        return halves.reshape(a.shape[:-1] + (2 * dff_pad,))

    w_in_b = _w_in_prep(jnp.swapaxes(w_in, 1, 2), mix_norm_w, 2 * cc + 3 * rd, nl, ad, 2 * akv, n_proj_pad)
    ffn_w_in_b = _cast_pad_halves(ffn_w_in, ffn_norm_w, dff, dff_pad)
    ffn_conv_w_p = pad_ff(ffn_conv_w)
    ffn_conv_b_p = pad_ff(ffn_conv_b)

    tm = _tile(s, 1024)
    bias = _attn_bias(rel_bias)
    h = x.reshape(s, d)
    hb, ssq = _h_prep(h, _tile(s, 512))
    for l in range(depth):
        proj, w_out_b = _matmul(hb, ssq, w_in_b, l, w_out, BF16, tm, 1024)
        y_conv = _conformer(proj, col_val, col_gate, conv_dw_w[l], conv_dw_b[l], conv_ln_w[l], conv_ln_b[l],
                            conv_pw_w[l], conv_pw_b[l], _tile(s, 256))
        y_rwkv = _rwkv(proj, col_r, col_k, col_v, col_l, rwkv_mu[l], rwkv_w_up[l], rwkv_w0[l], rwkv_a_up[l],
                       rwkv_a0[l], rwkv_g_up[l], rwkv_k_k[l], rwkv_k_a[l], rwkv_r_k[l].reshape(-1),
                       rwkv_lnx_w[l], rwkv_lnx_b[l], _tile(s, 256))
        y_att = _attention(proj, col_q, col_ak, col_av, nq, nkv, bias, attn_sinks[l], _tile(s, 512))
        h, hb, ssq = _mix_out(y_conv, y_rwkv, y_att, w_out_b, h, tm, 512)
        gact, ffn_w_out_b = _ffn_in(hb, ssq, ffn_w_in_b, l, ffn_conv_w_p[l], ffn_conv_b_p[l], ffn_w_out,
                                    dff_pad, tm, 512)
        h, hb, ssq = _ffn_out(gact, ffn_w_out_b, h, tm, 1024, dff_pad // 4)
    out = _rmsnorm(h, final_norm_w, F32, _tile(s, 512))
    return out.reshape(bsz, s, d)
```
